```python
import jax
import jax.numpy as jnp
from jax import lax
import numpy as np

D_MODEL = 1024
BATCH = 2
SEQ = 16384
DEPTH = 2

N_MEM = 256
D_BR = 512
N_BRANCH = 4
RW_HEADS = 8
RW_HD = D_BR // RW_HEADS
RW_RANK_W = 64
RW_RANK_A = 64
RW_LN_EPS = 64e-5
RW_SHIFT = 3 * D_BR + RW_RANK_W + RW_RANK_A
GLA_HEADS = 4
GLA_DK = D_BR // 2
GLA_RANK = 16
GLA_TAU = 16.0
GLA_CHUNK = 64
RET_HEADS = 4
RET_DK = D_BR // 2
RET_CHUNK = 64
ROPE_BASE = 10000.0
LRU_BLOCKS = 8
LRU_BW = D_BR // LRU_BLOCKS
LRU_CONV = 4
LRU_C = 8.0
XA_HEADS = 4
XA_HD = D_MODEL // XA_HEADS
NORM_EPS = 1e-6

IN_SPLITS = (
    ('rw_r', D_BR), ('rw_k', D_BR), ('rw_v', D_BR), ('rw_wlo', RW_RANK_W), ('rw_alo', RW_RANK_A), ('rw_g', D_BR),
    ('gla_q', GLA_DK), ('gla_k', GLA_DK), ('gla_v', D_BR), ('gla_flo', GLA_RANK), ('gla_g', D_BR),
    ('ret_q', RET_DK), ('ret_k', RET_DK), ('ret_v', D_BR), ('ret_g', D_BR),
    ('lru_x', D_BR), ('lru_g', D_BR),
    ('merge', N_BRANCH * D_MODEL),
)
N_IN = (RW_SHIFT + D_BR) + (2 * GLA_DK + 2 * D_BR + GLA_RANK) + (2 * RET_DK + 2 * D_BR) + 2 * D_BR + N_BRANCH * D_MODEL

kernel_name = 'hybrid_rwkv7_gla_retnet_rglru_gated_block'


def _rmsnorm(x, g):
    x32 = x.astype(jnp.float32)
    y = x32 * lax.rsqrt(jnp.mean(x32 * x32, -1, keepdims=True) + NORM_EPS)
    return (y * g.astype(jnp.float32)).astype(x.dtype)


def _split_last(z, sizes):
    return jnp.split(z, np.cumsum(sizes)[:-1].tolist(), axis=-1)


def _token_shift(z, mu):
    prev = jnp.pad(z[:, :-1], ((0, 0), (1, 0), (0, 0)))
    return z + (prev - z) * mu


def _rope(t, positions):
    half = t.shape[-1] // 2
    inv = ROPE_BASE ** (-jnp.arange(half, dtype=jnp.float32) / half)
    ang = positions.astype(jnp.float32)[..., None] * inv
    cos, sin = jnp.cos(ang)[:, :, None, :], jnp.sin(ang)[:, :, None, :]
    t1, t2 = t[..., :half], t[..., half:]
    return jnp.concatenate([t1 * cos - t2 * sin, t1 * sin + t2 * cos], -1)


def _rwkv7(r, k, v, w_lo, a_lo, w0, w2, a0, a2, k_k, k_a, r_k, ln_g, ln_b):
    B, S, _ = r.shape
    f32 = jnp.float32
    H, N = RW_HEADS, RW_HD
    log_w = -jax.nn.softplus(-(w0 + jnp.tanh(w_lo) @ w2).astype(f32)) - 0.5
    decay = jnp.exp(-jnp.exp(log_w))
    a = jax.nn.sigmoid((a0 + a_lo @ a2).astype(f32))
    k32 = k.astype(f32)
    kk = (k32 * k_k.astype(f32)).reshape(B, S, H, N)
    kk = kk / jnp.maximum(jnp.sqrt(jnp.sum(kk * kk, -1, keepdims=True)), 1e-12)
    k_mod = k32 * (1.0 + (a - 1.0) * k_a.astype(f32))
    heads = lambda t: t.astype(f32).reshape(B, S, H, N)
    rh, wh, kh, vh, ah = heads(r), heads(decay), heads(k_mod), heads(v), heads(a)
    bh = kk * ah

    def step(state, inp):
        r_t, w_t, k_t, v_t, kk_t, b_t = inp
        sa = jnp.einsum('bhvk,bhk->bhv', state, -kk_t)
        state = state * w_t[:, :, None, :] + sa[..., None] * b_t[:, :, None, :] + v_t[..., None] * k_t[:, :, None, :]
        return state, jnp.einsum('bhvk,bhk->bhv', state, r_t)

    xs = tuple(jnp.moveaxis(t, 1, 0) for t in (rh, wh, kh, vh, kk, bh))
    _, y = lax.scan(step, jnp.zeros((B, H, N, N), f32), xs)
    y = jnp.moveaxis(y, 0, 1)
    mean = jnp.mean(y, -1, keepdims=True)
    var = jnp.mean(jnp.square(y - mean), -1, keepdims=True)
    y = ((y - mean) * lax.rsqrt(var + RW_LN_EPS)).reshape(B, S, D_BR) * ln_g.astype(f32) + ln_b.astype(f32)
    bonus = jnp.sum(rh * kh * r_k.astype(f32), -1, keepdims=True) * vh
    return y + bonus.reshape(B, S, D_BR)


def _gla(q, k, v, f_lo, f_up, f_b, norm_g):
    B, S, _ = q.shape
    f32 = jnp.float32
    C, H = GLA_CHUNK, GLA_HEADS
    nC = S // C
    dk, dv = GLA_DK // H, D_BR // H
    log_f = jax.nn.log_sigmoid((f_lo @ f_up + f_b).astype(f32)) / GLA_TAU
    ch = lambda t, d: t.astype(f32).reshape(B, nC, C, H, d)
    q = ch(q, dk) * dk ** -0.5
    k = ch(k, dk)
    v = ch(v, dv)
    b = jnp.cumsum(ch(log_f, dk), axis=2)
    b_last = b[:, :, -1:]
    q_d = q * jnp.exp(b)
    k_d = k * jnp.exp(-b)
    k_end = k * jnp.exp(b_last - b)
    causal = jnp.tril(jnp.ones((C, C), dtype=bool))
    att = jnp.where(causal, jnp.einsum('bnthd,bnshd->bnhts', q_d, k_d), 0.0)
    o = jnp.einsum('bnhts,bnshv->bnthv', att, v)
    ds = jnp.einsum('bnshd,bnshv->bnhdv', k_end, v)
    dec = jnp.exp(b_last[:, :, 0])

    def step(state, inp):
        dec_n, ds_n = inp
        return state * dec_n[..., None] + ds_n, state

    _, s_prev = lax.scan(step, jnp.zeros((B, H, dk, dv), f32), (jnp.moveaxis(dec, 1, 0), jnp.moveaxis(ds, 1, 0)))
    s_prev = jnp.moveaxis(s_prev, 0, 1)
    o = o + jnp.einsum('bnthd,bnhdv->bnthv', q_d, s_prev)
    o = o * lax.rsqrt(jnp.mean(o * o, -1, keepdims=True) + NORM_EPS)
    return o.reshape(B, S, D_BR) * norm_g.astype(f32)


def _retention(q, k, v, positions, gn_g):
    B, S, _ = q.shape
    f32 = jnp.float32
    C, H = RET_CHUNK, RET_HEADS
    nC = S // C
    dk, dv = RET_DK // H, D_BR // H
    q = _rope(q.astype(f32).reshape(B, S, H, dk), positions)
    k = _rope(k.astype(f32).reshape(B, S, H, dk), positions) * dk ** -0.5
    q = q.reshape(B, nC, C, H, dk)
    k = k.reshape(B, nC, C, H, dk)
    v = v.astype(f32).reshape(B, nC, C, H, dv)
    log_g = jnp.log(1.0 - jnp.exp2(-5.0 - jnp.arange(H, dtype=f32)))
    idx = jnp.arange(C, dtype=f32)
    diff = idx[:, None] - idx[None, :]
    dmat = jnp.where(diff >= 0, jnp.exp(jnp.maximum(diff, 0.0)[None] * log_g[:, None, None]), 0.0)
    att = jnp.einsum('bnthd,bnshd->bnhts', q, k) * dmat
    o = jnp.einsum('bnhts,bnshv->bnthv', att, v)
    k_w = jnp.exp((C - 1.0 - idx)[:, None] * log_g)
    q_w = jnp.exp((idx + 1.0)[:, None] * log_g)
    ds = jnp.einsum('bnshd,sh,bnshv->bnhdv', k, k_w, v)
    chunk_dec = jnp.exp(C * log_g)[None, :, None, None]

    def step(state, ds_n):
        return state * chunk_dec + ds_n, state

    _, s_prev = lax.scan(step, jnp.zeros((B, H, dk, dv), f32), jnp.moveaxis(ds, 1, 0))
    s_prev = jnp.moveaxis(s_prev, 0, 1)
    o = o + jnp.einsum('bnthd,bnhdv->bnthv', q, s_prev) * q_w[:, :, None]
    mean = jnp.mean(o, -1, keepdims=True)
    var = jnp.mean(jnp.square(o - mean), -1, keepdims=True)
    o = (o - mean) * lax.rsqrt(var + NORM_EPS)
    return o.reshape(B, S, D_BR) * gn_g.astype(f32)


def _rglru(xb, conv_w, conv_b, wa, ba, wx, bx, lam):
    B, S, _ = xb.shape
    f32 = jnp.float32
    xc = lax.conv_general_dilated(xb, conv_w[:, None, :], window_strides=(1,), padding=[(LRU_CONV - 1, 0)],
                                  dimension_numbers=('NWC', 'WIO', 'NWC'), feature_group_count=D_BR) + conv_b
    xh = xc.reshape(B, S, LRU_BLOCKS, LRU_BW)
    r = jax.nn.sigmoid((jnp.einsum('bsnc,ncd->bsnd', xh, wa).reshape(B, S, D_BR) + ba).astype(f32))
    i = jax.nn.sigmoid((jnp.einsum('bsnc,ncd->bsnd', xh, wx).reshape(B, S, D_BR) + bx).astype(f32))
    log_a = -LRU_C * r * jax.nn.softplus(-lam.astype(f32))
    a = jnp.exp(log_a)
    u = jnp.sqrt(-jnp.expm1(2.0 * log_a)) * (i * xc.astype(f32))

    def combine(left, right):
        a1, b1 = left
        a2, b2 = right
        return a1 * a2, a2 * b1 + b2

    _, h = lax.associative_scan(combine, (a, u), axis=1)
    return h


def _mixer_sublayer(un, positions, w_in, rw_mu, rw_w0, rw_w2, rw_a0, rw_a2, rw_k_k, rw_k_a, rw_r_k,
                    rw_ln_g, rw_ln_b, gla_f_up, gla_f_b, gla_norm_g, ret_gn_g, lru_conv_w, lru_conv_b,
                    lru_wa, lru_ba, lru_wx, lru_bx, lru_lambda, w_branch, w_out):
    B, S, _ = un.shape
    f32 = jnp.float32
    z = un @ w_in
    p = dict(zip([n for n, _ in IN_SPLITS], _split_last(z, [w for _, w in IN_SPLITS])))
    mu_r, mu_k, mu_v, mu_w, mu_a = _split_last(rw_mu, [D_BR, D_BR, D_BR, RW_RANK_W, RW_RANK_A])

    y_rw = _rwkv7(_token_shift(p['rw_r'], mu_r), _token_shift(p['rw_k'], mu_k), _token_shift(p['rw_v'], mu_v),
                  _token_shift(p['rw_wlo'], mu_w), _token_shift(p['rw_alo'], mu_a),
                  rw_w0, rw_w2, rw_a0, rw_a2, rw_k_k, rw_k_a, rw_r_k, rw_ln_g, rw_ln_b)
    y_rw = y_rw * jax.nn.silu(p['rw_g'].astype(f32))
    y_gla = _gla(p['gla_q'], p['gla_k'], p['gla_v'], p['gla_flo'], gla_f_up, gla_f_b, gla_norm_g)
    y_gla = y_gla * jax.nn.silu(p['gla_g'].astype(f32))
    y_ret = _retention(p['ret_q'], p['ret_k'], p['ret_v'], positions, ret_gn_g)
    y_ret = y_ret * jax.nn.silu(p['ret_g'].astype(f32))
    y_lru = _rglru(p['lru_x'], lru_conv_w, lru_conv_b, lru_wa, lru_ba, lru_wx, lru_bx, lru_lambda)
    y_lru = y_lru * jax.nn.silu(p['lru_g'].astype(f32))

    gates = p['merge'].reshape(B, S, N_BRANCH, D_MODEL)
    branches = (y_rw, y_gla, y_ret, y_lru)
    merged = jnp.zeros_like(un)
    for n in range(N_BRANCH):
        merged = merged + jax.nn.sigmoid(gates[:, :, n]) * (branches[n].astype(un.dtype) @ w_branch[n])
    return merged @ w_out


def _cross_attn(xn, mn, wq, wkv, wo):
    B, S, _ = xn.shape
    M = mn.shape[1]
    q = (xn @ wq).reshape(B, S, XA_HEADS, XA_HD)
    k, v = jnp.split(mn @ wkv, 2, axis=-1)
    k = k.reshape(B, M, XA_HEADS, XA_HD)
    v = v.reshape(B, M, XA_HEADS, XA_HD)
    s = jnp.einsum('bshd,bmhd->bhsm', q, k).astype(jnp.float32) * XA_HD ** -0.5
    pr = jax.nn.softmax(s, axis=-1).astype(xn.dtype)
    o = jnp.einsum('bhsm,bmhd->bshd', pr, v).reshape(B, S, D_MODEL)
    return o @ wo


def setup_inputs(seed: int = 0) -> dict:
    key = jax.random.key(seed)
    k = jax.random.split(key, 40)
    f32 = jnp.float32
    nrm = lambda kk, shape, scale: scale * jax.random.normal(kk, shape, f32)
    gain = lambda kk, shape: 1.0 + 0.02 * jax.random.normal(kk, shape, f32)
    L = DEPTH
    x = jax.random.normal(k[0], (BATCH, SEQ, D_MODEL), f32)
    mem = jax.random.normal(k[1], (BATCH, N_MEM, D_MODEL), f32)
    positions = (jnp.arange(SEQ, dtype=jnp.int32)[None, :]
                 + jax.random.randint(k[2], (BATCH, 1), 0, 1024, dtype=jnp.int32))
    a_init = jax.random.uniform(k[3], (L, D_BR), f32, 0.9, 0.999)
    s_init = a_init ** (1.0 / LRU_C)
    lru_lambda = jnp.log(s_init) - jnp.log1p(-s_init)
    return {
        'x': x,
        'mem': mem,
        'positions': positions,
        'mix_norm_g': gain(k[4], (L, D_MODEL)),
        'w_in': nrm(k[5], (L, D_MODEL, N_IN), D_MODEL ** -0.5),
        'rw_mu': jax.random.uniform(k[6], (L, RW_SHIFT), f32),
        'rw_w0': nrm(k[7], (L, D_BR), 0.5),
        'rw_w2': nrm(k[8], (L, RW_RANK_W, D_BR), RW_RANK_W ** -0.5),
        'rw_a0': nrm(k[9], (L, D_BR), 0.5),
        'rw_a2': nrm(k[10], (L, RW_RANK_A, D_BR), RW_RANK_A ** -0.5),
        'rw_k_k': 1.0 + nrm(k[11], (L, D_BR), 0.1),
        'rw_k_a': 1.0 + nrm(k[12], (L, D_BR), 0.1),
        'rw_r_k': nrm(k[13], (L, RW_HEADS, RW_HD), 0.1),
        'rw_ln_g': gain(k[14], (L, D_BR)),
        'rw_ln_b': nrm(k[15], (L, D_BR), 0.01),
        'gla_f_up': nrm(k[16], (L, GLA_RANK, GLA_DK), GLA_RANK ** -0.5),
        'gla_f_b': nrm(k[17], (L, GLA_DK), 0.1),
        'gla_norm_g': gain(k[18], (L, D_BR)),
        'ret_gn_g': gain(k[19], (L, D_BR)),
        'lru_conv_w': nrm(k[20], (L, LRU_CONV, D_BR), LRU_CONV ** -0.5),
        'lru_conv_b': nrm(k[21], (L, D_BR), 0.01),
        'lru_wa': nrm(k[22], (L, LRU_BLOCKS, LRU_BW, LRU_BW), LRU_BW ** -0.5),
        'lru_ba': nrm(k[23], (L, D_BR), 0.01),
        'lru_wx': nrm(k[24], (L, LRU_BLOCKS, LRU_BW, LRU_BW), LRU_BW ** -0.5),
        'lru_bx': nrm(k[25], (L, D_BR), 0.01),
        'lru_lambda': lru_lambda,
        'w_branch': nrm(k[26], (L, N_BRANCH, D_BR, D_MODEL), D_BR ** -0.5),
        'w_out': nrm(k[27], (L, D_MODEL, D_MODEL), D_MODEL ** -0.5),
        'xa_norm_g': gain(k[28], (L, D_MODEL)),
        'xa_mem_norm_g': gain(k[29], (L, D_MODEL)),
        'xa_wq': nrm(k[30], (L, D_MODEL, D_MODEL), D_MODEL ** -0.5),
        'xa_wkv': nrm(k[31], (L, D_MODEL, 2 * D_MODEL), D_MODEL ** -0.5),
        'xa_wo': nrm(k[32], (L, D_MODEL, D_MODEL), D_MODEL ** -0.5),
        'final_norm_g': gain(k[33], (D_MODEL,)),
    }


def reference(x, mem, positions, mix_norm_g, w_in, rw_mu, rw_w0, rw_w2, rw_a0, rw_a2, rw_k_k, rw_k_a,
              rw_r_k, rw_ln_g, rw_ln_b, gla_f_up, gla_f_b, gla_norm_g, ret_gn_g, lru_conv_w, lru_conv_b,
              lru_wa, lru_ba, lru_wx, lru_bx, lru_lambda, w_branch, w_out, xa_norm_g, xa_mem_norm_g,
              xa_wq, xa_wkv, xa_wo, final_norm_g):
    h = x
    for l in range(DEPTH):
        un = _rmsnorm(h, mix_norm_g[l])
        h = h + _mixer_sublayer(un, positions, w_in[l], rw_mu[l], rw_w0[l], rw_w2[l], rw_a0[l], rw_a2[l],
                                rw_k_k[l], rw_k_a[l], rw_r_k[l], rw_ln_g[l], rw_ln_b[l], gla_f_up[l],
                                gla_f_b[l], gla_norm_g[l], ret_gn_g[l], lru_conv_w[l], lru_conv_b[l],
                                lru_wa[l], lru_ba[l], lru_wx[l], lru_bx[l], lru_lambda[l], w_branch[l], w_out[l])
        xn = _rmsnorm(h, xa_norm_g[l])
        mn = _rmsnorm(mem, xa_mem_norm_g[l])
        h = h + _cross_attn(xn, mn, xa_wq[l], xa_wkv[l], xa_wo[l])
    return _rmsnorm(h, final_norm_g)
```

```python
import functools

import numpy as np
import jax
import jax.numpy as jnp
from jax import lax
from jax.experimental import pallas as pl
from jax.experimental.pallas import tpu as pltpu

F32 = jnp.float32
BF16 = jnp.bfloat16

D_MODEL = 1024
D_BR = 512
N_BRANCH = 4
RW_HEADS = 8
RW_HD = D_BR // RW_HEADS
RW_RANK = 64
RW_LN_EPS = 64e-5
RW_SHIFT = 3 * D_BR + 2 * RW_RANK
GLA_DK = D_BR // 2
GLA_RANK = 16
GLA_TAU = 16.0
RET_DK = D_BR // 2
ROPE_BASE = 10000.0
ROPE_HALF = 32
LRU_BLOCKS = 8
LRU_CONV = 4
LRU_C = 8.0
XA_HEADS = 4
XA_HD = D_MODEL // XA_HEADS
NORM_EPS = 1e-6
CHUNK = 64
LANES = 128
CARRY_ROWS = 8
MAX_TIME_TILE = 512
VMEM_LIMIT_BYTES = 56 * 1024 * 1024


def _mm(a, b):
    return jnp.dot(a.astype(BF16), b.astype(BF16), preferred_element_type=F32)


def _mm_nt(a, b):
    return lax.dot_general(a.astype(BF16), b.astype(BF16), (((1,), (1,)), ((), ())),
                           preferred_element_type=F32)


def _mm_tn(a, b):
    return lax.dot_general(a.astype(BF16), b.astype(BF16), (((0,), (0,)), ((), ())),
                           preferred_element_type=F32)


def _split2(x):
    hi = x.astype(BF16)
    lo = (x - hi.astype(F32)).astype(BF16)
    return hi, lo


def _split3(x):
    hi = x.astype(BF16)
    r = x - hi.astype(F32)
    mid = r.astype(BF16)
    lo = (r - mid.astype(F32)).astype(BF16)
    return hi, mid, lo


def _mm_sel(sel_bf16, x):
    hi, mid, lo = _split3(x)
    return _mm(sel_bf16, hi) + (_mm(sel_bf16, mid) + _mm(sel_bf16, lo))


def _mm3(dot, a, b):
    ah, al = _split2(a)
    bh, bl = _split2(b)
    return dot(ah, bh) + (dot(ah, bl) + dot(al, bh))


def _rmsnorm(x, g):
    ms = jnp.mean(x * x, axis=-1, keepdims=True)
    return x * lax.rsqrt(ms + NORM_EPS) * g


def _sigmoid(x):
    return 1.0 / (1.0 + jnp.exp(-x))


def _softplus(x):
    return jnp.maximum(x, 0.0) + jnp.log(1.0 + jnp.exp(-jnp.abs(x)))


def _silu(x):
    return x * _sigmoid(x)


def _iota(shape, dim):
    return lax.broadcasted_iota(jnp.int32, shape, dim)


def _seg_sum(x, width):
    pieces = []
    for j in range(x.shape[-1] // LANES):
        xs = x[:, j * LANES:(j + 1) * LANES]
        if width == LANES:
            s = jnp.sum(xs, axis=-1, keepdims=True)
            pieces.append(jnp.broadcast_to(s, xs.shape))
        else:
            low = _iota(xs.shape, 1) < width
            s_lo = jnp.sum(jnp.where(low, xs, 0.0), axis=-1, keepdims=True)
            s_hi = jnp.sum(jnp.where(low, 0.0, xs), axis=-1, keepdims=True)
            pieces.append(jnp.where(low, s_lo, s_hi))
    return jnp.concatenate(pieces, axis=-1)


def _block_diag(x, n, row_blk, col_blk):
    t = jnp.concatenate([x] * n, axis=0)
    keep = (_iota(t.shape, 0) // row_blk) == (_iota(t.shape, 1) // col_blk)
    return jnp.where(keep, t, jnp.zeros_like(t))


def _block_mask(shape, row_blk, col_blk):
    return (_iota(shape, 0) // row_blk) == (_iota(shape, 1) // col_blk)


def _chunk_selectors(ts):
    r = _iota((ts, ts), 0)
    c = _iota((ts, ts), 1)
    same = (r // CHUNK) == (c // CHUNK)
    tri = jnp.where(same & (c <= r), 1.0, 0.0).astype(BF16)
    ones = jnp.where(same, 1.0, 0.0).astype(BF16)
    return tri, ones


def _const_spec(shape):
    nd = len(shape)
    return pl.BlockSpec(shape, lambda *_: (0,) * nd, pipeline_mode=pl.Buffered(1))


def _time_tile(s):
    ts = min(MAX_TIME_TILE, s)
    assert s % ts == 0 and ts % CHUNK == 0
    return ts


def _compiler_params(n_axes):
    return pltpu.CompilerParams(dimension_semantics=("arbitrary",) * n_axes,
                                vmem_limit_bytes=VMEM_LIMIT_BYTES)


def _rwkv_kernel(h_ref, g_ref, w_ref, mu_ref, wlr_ref, w0a0_ref, kk_ref, ka_ref, rk_ref, lng_ref, lnb_ref,
                 o_ref, zbuf, ht_ref, kt_ref, rt_ref, kd_ref, bd_ref, ke_ref, be_ref, v_ref, dec_ref, y_ref):
    ts = h_ref.shape[0]
    hw = 4 * RW_HD

    @pl.when(pl.program_id(1) == 0)
    def _():
        zbuf[0:CARRY_ROWS, :] = jnp.zeros((CARRY_ROWS, RW_SHIFT), F32)
        ht_ref[...] = jnp.zeros(ht_ref.shape, F32)

    un = _rmsnorm(h_ref[...], g_ref[...])
    z = _mm(un, w_ref[...])
    zs = z[:, :RW_SHIFT]
    gate = z[:, RW_SHIFT:]
    zbuf[CARRY_ROWS:CARRY_ROWS + ts, :] = zs
    prev = zbuf[CARRY_ROWS - 1:CARRY_ROWS - 1 + ts, :]
    zbuf[0:CARRY_ROWS, :] = zbuf[ts:ts + CARRY_ROWS, :]
    sh = zs + (prev - zs) * mu_ref[...]
    r = sh[:, 0:D_BR]
    k = sh[:, D_BR:2 * D_BR]
    v = sh[:, 2 * D_BR:3 * D_BR]
    lr = sh[:, 3 * D_BR:]
    lr = jnp.where(_iota(lr.shape, 1) < RW_RANK, jnp.tanh(lr), lr)
    pre = _mm(lr, wlr_ref[...]) + w0a0_ref[...]
    log_w = -_softplus(-pre[:, :D_BR]) - 0.5
    lw = -jnp.exp(log_w)
    a = _sigmoid(pre[:, D_BR:])
    kk = k * kk_ref[...]
    kk = kk / jnp.maximum(jnp.sqrt(_seg_sum(kk * kk, RW_HD)), 1e-12)
    k_mod = k * (1.0 + (a - 1.0) * ka_ref[...])
    b = kk * a
    bonus = _seg_sum(r * k_mod * rk_ref[...], RW_HD) * v

    tri, ones = _chunk_selectors(ts)
    c = _mm_sel(tri, lw)
    c_end = _mm_sel(ones, lw)
    inv = jnp.exp(-c)
    to_end = jnp.exp(c_end - c)
    kt_ref[...] = kk * jnp.exp(c - lw)
    rt_ref[...] = r * jnp.exp(c)
    kd_ref[...] = k_mod * inv
    bd_ref[...] = b * inv
    ke_ref[...] = k_mod * to_end
    be_ref[...] = b * to_end
    v_ref[...] = v
    dec_ref[...] = jnp.exp(c_end)

    col = _iota((CHUNK, hw), 1) % CHUNK
    row = _iota((CHUNK, hw), 0)
    strict = col < row
    incl = col <= row
    eye = jnp.where(col == row, 1.0, 0.0)
    blk = _block_mask((hw, hw), RW_HD, RW_HD)
    mm3 = functools.partial(_mm3, _mm)
    mm3_nt = functools.partial(_mm3, _mm_nt)
    mm3_tn = functools.partial(_mm3, _mm_tn)
    bdg = functools.partial(_block_diag, n=4, row_blk=CHUNK, col_blk=RW_HD)

    def chunk_body(ci, carry):
        rows = pl.ds(pl.multiple_of(ci * CHUNK, CHUNK), CHUNK)
        for gi in range(RW_HEADS // 4):
            ls = slice(hw * gi, hw * (gi + 1))
            kt = kt_ref[rows, ls]
            rt = rt_ref[rows, ls]
            vv = v_ref[rows, ls]
            lhs = jnp.concatenate([kt, rt], axis=0)
            sb = mm3_nt(lhs, bdg(bd_ref[rows, ls]))
            sk = mm3_nt(lhs, bdg(kd_ref[rows, ls]))
            l_b = jnp.where(strict, sb[:CHUNK], 0.0)
            a_b = jnp.where(incl, sb[CHUNK:], 0.0)
            l_k = jnp.where(strict, sk[:CHUNK], 0.0)
            a_k = jnp.where(incl, sk[CHUNK:], 0.0)
            m = -l_b
            p = eye + m
            for _ in range(5):
                m2 = mm3(m, bdg(m))
                p = p + mm3(p, bdg(m2))
                m = m2
            lkv = mm3(l_k, bdg(vv))
            wm = mm3(p, bdg(kt))
            u0 = mm3(p, bdg(lkv))
            ht = ht_ref[gi]
            u = mm3_nt(wm, ht) + u0
            y = mm3_nt(rt, ht) + (mm3(a_k, bdg(vv)) - mm3(a_b, bdg(u)))
            y_ref[rows, ls] = y
            upd = mm3_tn(jnp.concatenate([vv, u], axis=0),
                         jnp.concatenate([ke_ref[rows, ls], -be_ref[rows, ls]], axis=0))
            ht_ref[gi] = ht * dec_ref[pl.ds(pl.multiple_of(ci * CHUNK, CHUNK), 1), ls] + jnp.where(blk, upd, 0.0)
        return carry

    lax.fori_loop(0, ts // CHUNK, chunk_body, 0)

    y = y_ref[...]
    mean = _seg_sum(y, RW_HD) * (1.0 / RW_HD)
    yc = y - mean
    var = _seg_sum(yc * yc, RW_HD) * (1.0 / RW_HD)
    y = yc * lax.rsqrt(var + RW_LN_EPS) * lng_ref[...] + lnb_ref[...]
    o_ref[...] = ((y + bonus) * _silu(gate)).astype(o_ref.dtype)


def _rwkv_branch(h, g, w, mu, wlr, w0a0, k_k, k_a, r_k, ln_g, ln_b):
    bsz, s, _ = h.shape
    ts = _time_tile(s)
    n_in = w.shape[1]
    vec = lambda n: _const_spec((1, n))
    big = pltpu.VMEM((ts, D_BR), F32)
    return pl.pallas_call(
        _rwkv_kernel,
        grid=(bsz, s // ts),
        in_specs=[pl.BlockSpec((None, ts, D_MODEL), lambda b, t: (b, t, 0)),
                  vec(D_MODEL), _const_spec((D_MODEL, n_in)), vec(RW_SHIFT),
                  _const_spec((2 * RW_RANK, 2 * D_BR)), vec(2 * D_BR),
                  vec(D_BR), vec(D_BR), vec(D_BR), vec(D_BR), vec(D_BR)],
        out_specs=pl.BlockSpec((None, ts, D_BR), lambda b, t: (b, t, 0)),
        out_shape=jax.ShapeDtypeStruct((bsz, s, D_BR), BF16),
        scratch_shapes=[pltpu.VMEM((ts + CARRY_ROWS, RW_SHIFT), F32),
                        pltpu.VMEM((RW_HEADS // 4, 4 * RW_HD, 4 * RW_HD), F32),
                        big, big, big, big, big, big, big, big, big],
        compiler_params=_compiler_params(2),
        name="rwkv_branch",
    )(h, g, w, mu, wlr, w0a0, k_k, k_a, r_k, ln_g, ln_b)


def _gla_kernel(h_ref, g_ref, w_ref, fup_ref, fb_ref, ng_ref, o_ref, st_ref, qd_ref, kd_ref, ke_ref, v_ref, dec_ref, y_ref):
    ts = h_ref.shape[0]
    dk = GLA_DK // 4
    dv = D_BR // 4

    @pl.when(pl.program_id(1) == 0)
    def _():
        st_ref[...] = jnp.zeros(st_ref.shape, F32)

    un = _rmsnorm(h_ref[...], g_ref[...])
    z = _mm(un, w_ref[...])
    q = z[:, 0:GLA_DK]
    k = z[:, GLA_DK:2 * GLA_DK]
    v = z[:, 2 * GLA_DK:2 * GLA_DK + D_BR]
    gate = z[:, 2 * GLA_DK + D_BR:2 * GLA_DK + 2 * D_BR]
    f_lo = z[:, 2 * GLA_DK + 2 * D_BR:]
    f = _mm(f_lo, fup_ref[...]) + fb_ref[...]
    log_f = -_softplus(-f) * (1.0 / GLA_TAU)
    tri, ones = _chunk_selectors(ts)
    bcum = _mm_sel(tri, log_f)
    b_end = _mm_sel(ones, log_f)
    qd_ref[...] = q * (dk ** -0.5) * jnp.exp(bcum)
    kd_ref[...] = k * jnp.exp(-bcum)
    ke_ref[...] = k * jnp.exp(b_end - bcum)
    v_ref[...] = v
    dec_ref[...] = jnp.exp(b_end)

    causal = (_iota((CHUNK, 4 * CHUNK), 1) % CHUNK) <= _iota((CHUNK, 4 * CHUNK), 0)
    st_mask = _block_mask((D_BR, GLA_DK), dv, dk)

    def chunk_body(ci, carry):
        r0 = pl.multiple_of(ci * CHUNK, CHUNK)
        rows = pl.ds(r0, CHUNK)
        qd = qd_ref[rows, :]
        vv = v_ref[rows, :]
        att = _mm_nt(qd, _block_diag(kd_ref[rows, :], 4, CHUNK, dk))
        att = jnp.where(causal, att, 0.0)
        st = st_ref[...]
        y_ref[rows, :] = _mm(att, _block_diag(vv, 4, CHUNK, dv)) + _mm_nt(qd, st)
        upd = _mm_tn(vv, ke_ref[rows, :])
        st_ref[...] = st * dec_ref[pl.ds(r0, 1), :] + jnp.where(st_mask, upd, 0.0)
        return carry

    lax.fori_loop(0, ts // CHUNK, chunk_body, 0)

    o = y_ref[...]
    o = o * lax.rsqrt(_seg_sum(o * o, dv) * (1.0 / dv) + NORM_EPS)
    o_ref[...] = (o * ng_ref[...] * _silu(gate)).astype(o_ref.dtype)


def _gla_branch(h, g, w, f_up, f_b, norm_g):
    bsz, s, _ = h.shape
    ts = _time_tile(s)
    vec = lambda n: _const_spec((1, n))
    return pl.pallas_call(
        _gla_kernel,
        grid=(bsz, s // ts),
        in_specs=[pl.BlockSpec((None, ts, D_MODEL), lambda b, t: (b, t, 0)),
                  vec(D_MODEL), _const_spec(w.shape), _const_spec(f_up.shape), vec(GLA_DK), vec(D_BR)],
        out_specs=pl.BlockSpec((None, ts, D_BR), lambda b, t: (b, t, 0)),
        out_shape=jax.ShapeDtypeStruct((bsz, s, D_BR), BF16),
        scratch_shapes=[pltpu.VMEM((D_BR, GLA_DK), F32),
                        pltpu.VMEM((ts, GLA_DK), F32), pltpu.VMEM((ts, GLA_DK), F32),
                        pltpu.VMEM((ts, GLA_DK), F32), pltpu.VMEM((ts, D_BR), F32),
                        pltpu.VMEM((ts, GLA_DK), F32), pltpu.VMEM((ts, D_BR), F32)],
        compiler_params=_compiler_params(2),
        name="gla_branch",
    )(h, g, w, f_up, f_b, norm_g)


def _ret_kernel(h_ref, pos_ref, g_ref, w_ref, gn_ref, freq_ref, dmat_ref, qw_ref, kw_ref, cdec_ref,
                o_ref, st_ref, q_ref, k_ref, ke_ref, v_ref, y_ref):
    ts = h_ref.shape[0]
    dk = RET_DK // 4
    dv = D_BR // 4

    @pl.when(pl.program_id(1) == 0)
    def _():
        st_ref[...] = jnp.zeros(st_ref.shape, F32)

    un = _rmsnorm(h_ref[...], g_ref[...])
    z = _mm(un, w_ref[...])
    q = z[:, 0:RET_DK]
    k = z[:, RET_DK:2 * RET_DK]
    v = z[:, 2 * RET_DK:2 * RET_DK + D_BR]
    gate = z[:, 2 * RET_DK + D_BR:]

    lane = _iota((ts, LANES), 1)
    ang = pos_ref[...].astype(F32) * freq_ref[...]
    first = (lane % (2 * ROPE_HALF)) < ROPE_HALF
    cos1 = jnp.cos(ang)
    sin1 = jnp.where(first, -jnp.sin(ang), jnp.sin(ang))
    cos2 = jnp.concatenate([cos1, cos1], axis=-1)
    sin2 = jnp.concatenate([sin1, sin1], axis=-1)
    first2 = jnp.concatenate([first, first], axis=-1)

    def rope(t):
        partner = jnp.where(first2, pltpu.roll(t, RET_DK - ROPE_HALF, 1), pltpu.roll(t, ROPE_HALF, 1))
        return t * cos2 + partner * sin2

    q_ref[...] = rope(q)
    kr = rope(k) * (dk ** -0.5)
    k_ref[...] = kr
    ke_ref[...] = kr * jnp.concatenate([kw_ref[...]] * (ts // CHUNK), axis=0)
    v_ref[...] = v
    st_mask = _block_mask((D_BR, RET_DK), dv, dk)

    def chunk_body(ci, carry):
        rows = pl.ds(pl.multiple_of(ci * CHUNK, CHUNK), CHUNK)
        qq = q_ref[rows, :]
        vv = v_ref[rows, :]
        att = _mm_nt(qq, _block_diag(k_ref[rows, :], 4, CHUNK, dk)) * dmat_ref[...]
        st = st_ref[...]
        y_ref[rows, :] = _mm(att, _block_diag(vv, 4, CHUNK, dv)) + _mm_nt(qq, st) * qw_ref[...]
        upd = _mm_tn(vv, ke_ref[rows, :])
        st_ref[...] = st * cdec_ref[...] + jnp.where(st_mask, upd, 0.0)
        return carry

    lax.fori_loop(0, ts // CHUNK, chunk_body, 0)

    o = y_ref[...]
    mean = _seg_sum(o, dv) * (1.0 / dv)
    oc = o - mean
    var = _seg_sum(oc * oc, dv) * (1.0 / dv)
    o = oc * lax.rsqrt(var + NORM_EPS)
    o_ref[...] = (o * gn_ref[...] * _silu(gate)).astype(o_ref.dtype)


def _ret_tables():
    h = 4
    dk = RET_DK // h
    dv = D_BR // h
    log_g = jnp.log(1.0 - jnp.exp2(-5.0 - jnp.arange(h, dtype=F32)))
    idx = jnp.arange(CHUNK, dtype=F32)
    diff = idx[:, None] - idx[None, :]
    dmat = jnp.where(diff >= 0, jnp.exp(jnp.maximum(diff, 0.0)[None] * log_g[:, None, None]), 0.0)
    dmat = jnp.transpose(dmat, (1, 0, 2)).reshape(CHUNK, h * CHUNK)
    q_w = jnp.exp((idx + 1.0)[:, None] * log_g)
    k_w = jnp.exp((CHUNK - 1.0 - idx)[:, None] * log_g)
    cdec = jnp.exp(CHUNK * log_g)
    freq = ROPE_BASE ** (-jnp.arange(ROPE_HALF, dtype=F32) / ROPE_HALF)
    return (jnp.tile(freq, LANES // ROPE_HALF)[None, :], dmat, jnp.repeat(q_w, dv, axis=1),
            jnp.repeat(k_w, dk, axis=1), jnp.repeat(cdec, dk)[None, :])


def _ret_branch(h, pos, g, w, gn_g):
    bsz, s, _ = h.shape
    ts = _time_tile(s)
    freq, dmat, q_w, k_w, cdec = _ret_tables()
    vec = lambda n: _const_spec((1, n))
    return pl.pallas_call(
        _ret_kernel,
        grid=(bsz, s // ts),
        in_specs=[pl.BlockSpec((None, ts, D_MODEL), lambda b, t: (b, t, 0)),
                  pl.BlockSpec((None, ts, 1), lambda b, t: (b, t, 0)),
                  vec(D_MODEL), _const_spec(w.shape), vec(D_BR), vec(LANES),
                  _const_spec(dmat.shape), _const_spec(q_w.shape), _const_spec(k_w.shape), _const_spec(cdec.shape)],
        out_specs=pl.BlockSpec((None, ts, D_BR), lambda b, t: (b, t, 0)),
        out_shape=jax.ShapeDtypeStruct((bsz, s, D_BR), BF16),
        scratch_shapes=[pltpu.VMEM((D_BR, RET_DK), F32),
                        pltpu.VMEM((ts, RET_DK), F32), pltpu.VMEM((ts, RET_DK), F32),
                        pltpu.VMEM((ts, RET_DK), F32), pltpu.VMEM((ts, D_BR), F32),
                        pltpu.VMEM((ts, D_BR), F32)],
        compiler_params=_compiler_params(2),
        name="ret_branch",
    )(h, pos, g, w, gn_g, freq, dmat, q_w, k_w, cdec)


def _lru_kernel(h_ref, g_ref, w_ref, cw_ref, cb_ref, wax_ref, bax_ref, lam_ref, o_ref, xbuf, hc_ref):
    ts = h_ref.shape[0]

    @pl.when(pl.program_id(1) == 0)
    def _():
        xbuf[0:CARRY_ROWS, :] = jnp.zeros((CARRY_ROWS, D_BR), F32)
        hc_ref[...] = jnp.zeros(hc_ref.shape, F32)

    un = _rmsnorm(h_ref[...], g_ref[...])
    z = _mm(un, w_ref[...])
    gate = z[:, D_BR:]
    xbuf[CARRY_ROWS:CARRY_ROWS + ts, :] = z[:, :D_BR]
    xc = cb_ref[...] + jnp.zeros((ts, D_BR), F32)
    for j in range(LRU_CONV):
        off = CARRY_ROWS - (LRU_CONV - 1) + j
        xc = xc + xbuf[off:off + ts, :] * cw_ref[j:j + 1, :]
    xbuf[0:CARRY_ROWS, :] = xbuf[ts:ts + CARRY_ROWS, :]

    ri = _sigmoid(_mm(xc, wax_ref[...]) + bax_ref[...])
    log_a = -LRU_C * ri[:, :D_BR] * _softplus(-lam_ref[...])
    a = jnp.exp(log_a)
    u = jnp.sqrt(1.0 - jnp.exp(2.0 * log_a)) * (ri[:, D_BR:] * xc)

    row = _iota((ts, D_BR), 0)
    d = 1
    while d < ts:
        keep = row >= d
        u = jnp.where(keep, a * pltpu.roll(u, d, 0) + u, u)
        a = jnp.where(keep, a * pltpu.roll(a, d, 0), a)
        d *= 2
    hh = u + a * hc_ref[0:1, :]
    hc_ref[...] = jnp.broadcast_to(hh[ts - 1:ts, :], hc_ref.shape)
    o_ref[...] = (hh * _silu(gate)).astype(o_ref.dtype)


def _lru_branch(h, g, w, conv_w, conv_b, wax, bax, lam):
    bsz, s, _ = h.shape
    ts = _time_tile(s)
    vec = lambda n: _const_spec((1, n))
    return pl.pallas_call(
        _lru_kernel,
        grid=(bsz, s // ts),
        in_specs=[pl.BlockSpec((None, ts, D_MODEL), lambda b, t: (b, t, 0)),
                  vec(D_MODEL), _const_spec(w.shape), _const_spec(conv_w.shape), vec(D_BR),
                  _const_spec(wax.shape), vec(2 * D_BR), vec(D_BR)],
        out_specs=pl.BlockSpec((None, ts, D_BR), lambda b, t: (b, t, 0)),
        out_shape=jax.ShapeDtypeStruct((bsz, s, D_BR), BF16),
        scratch_shapes=[pltpu.VMEM((ts + CARRY_ROWS, D_BR), F32), pltpu.VMEM((CARRY_ROWS, D_BR), F32)],
        compiler_params=_compiler_params(2),
        name="lru_branch",
    )(h, g, w, conv_w, conv_b, wax, bax, lam)


def _merge_kernel(h_ref, y0_ref, y1_ref, y2_ref, y3_ref, g_ref, wm_ref, wb_ref, wo_ref, o_ref):
    x = h_ref[...]
    un = _rmsnorm(x, g_ref[...]).astype(BF16)
    merged = None
    for n, y_ref in enumerate((y0_ref, y1_ref, y2_ref, y3_ref)):
        gate = _sigmoid(_mm(un, wm_ref[:, n * D_MODEL:(n + 1) * D_MODEL]))
        term = gate * _mm(y_ref[...], wb_ref[n])
        merged = term if merged is None else merged + term
    o_ref[...] = x + _mm(merged, wo_ref[...])


def _merge(h, ys, g, w_merge, w_branch, w_out):
    bsz, s, _ = h.shape
    ts = _time_tile(s)
    vec = lambda n: _const_spec((1, n))
    tile = lambda n: pl.BlockSpec((None, ts, n), lambda b, t: (b, t, 0))
    return pl.pallas_call(
        _merge_kernel,
        grid=(bsz, s // ts),
        in_specs=[tile(D_MODEL), tile(D_BR), tile(D_BR), tile(D_BR), tile(D_BR),
                  vec(D_MODEL), _const_spec(w_merge.shape), _const_spec(w_branch.shape), _const_spec(w_out.shape)],
        out_specs=tile(D_MODEL),
        out_shape=jax.ShapeDtypeStruct(h.shape, F32),
        compiler_params=_compiler_params(2),
        name="merge_out",
    )(h, *ys, g, w_merge, w_branch, w_out)


def _xattn_kernel(h_ref, mem_ref, g_ref, gm_ref, wq_ref, wkv_ref, wo_ref, gf_ref, o_ref, k_ref, v_ref, *, final_norm):
    @pl.when(pl.program_id(1) == 0)
    def _():
        mn = _rmsnorm(mem_ref[...], gm_ref[...])
        kv = _mm(mn, wkv_ref[...])
        k_ref[...] = kv[:, :D_MODEL].astype(BF16)
        v_ref[...] = kv[:, D_MODEL:].astype(BF16)

    x = h_ref[...]
    q = _mm(_rmsnorm(x, g_ref[...]), wq_ref[...])
    outs = []
    for hd in range(XA_HEADS):
        ls = slice(hd * XA_HD, (hd + 1) * XA_HD)
        sc = _mm_nt(q[:, ls], k_ref[:, ls]) * (XA_HD ** -0.5)
        sc = sc - jnp.max(sc, axis=-1, keepdims=True)
        e = jnp.exp(sc)
        pr = e / jnp.sum(e, axis=-1, keepdims=True)
        outs.append(_mm(pr, v_ref[:, ls]))
    out = x + _mm(jnp.concatenate(outs, axis=-1), wo_ref[...])
    if final_norm:
        out = _rmsnorm(out, gf_ref[...])
    o_ref[...] = out


def _xattn(h, mem, g, g_mem, wq, wkv, wo, g_final, final_norm):
    bsz, s, _ = h.shape
    n_mem = mem.shape[1]
    ts = _time_tile(s)
    vec = lambda n: _const_spec((1, n))
    tile = pl.BlockSpec((None, ts, D_MODEL), lambda b, t: (b, t, 0))
    return pl.pallas_call(
        functools.partial(_xattn_kernel, final_norm=final_norm),
        grid=(bsz, s // ts),
        in_specs=[tile, pl.BlockSpec((None, n_mem, D_MODEL), lambda b, t: (b, 0, 0)),
                  vec(D_MODEL), vec(D_MODEL), _const_spec(wq.shape), _const_spec(wkv.shape), _const_spec(wo.shape),
                  vec(D_MODEL)],
        out_specs=tile,
        out_shape=jax.ShapeDtypeStruct(h.shape, F32),
        scratch_shapes=[pltpu.VMEM((n_mem, D_MODEL), BF16), pltpu.VMEM((n_mem, D_MODEL), BF16)],
        compiler_params=_compiler_params(2),
        name="xattn_final" if final_norm else "xattn",
    )(h, mem, g, g_mem, wq, wkv, wo, g_final)


def _row(v):
    return v.reshape(1, -1).astype(F32)


def _dense_block_diag(w):
    n, c, d = w.shape
    eye = jnp.eye(n, dtype=w.dtype)
    return (eye[:, None, :, None] * w[:, :, None, :]).reshape(n * c, n * d)


def kernel(x, mem, positions, mix_norm_g, w_in, rw_mu, rw_w0, rw_w2, rw_a0, rw_a2, rw_k_k, rw_k_a, rw_r_k, rw_ln_g, rw_ln_b, gla_f_up, gla_f_b, gla_norm_g, ret_gn_g, lru_conv_w, lru_conv_b, lru_wa, lru_ba, lru_wx, lru_bx, lru_lambda, w_branch, w_out, xa_norm_g, xa_mem_norm_g, xa_wq, xa_wkv, xa_wo, final_norm_g):
    depth = w_in.shape[0]
    pos = positions[..., None]
    n_rw = RW_SHIFT + D_BR
    n_gla = 2 * GLA_DK + 2 * D_BR + GLA_RANK
    n_ret = 2 * RET_DK + 2 * D_BR
    o_gla = n_rw
    o_ret = o_gla + n_gla
    o_lru = o_ret + n_ret
    o_mrg = o_lru + 2 * D_BR
    h = x
    for l in range(depth):
        wl = w_in[l].astype(BF16)
        g_mix = _row(mix_norm_g[l])

        z64 = jnp.zeros((RW_RANK, D_BR), F32)
        wlr = jnp.concatenate([jnp.concatenate([rw_w2[l], z64], axis=1),
                               jnp.concatenate([z64, rw_a2[l]], axis=1)], axis=0).astype(BF16)
        y_rw = _rwkv_branch(h, g_mix, wl[:, :n_rw], _row(rw_mu[l]), wlr,
                            _row(jnp.concatenate([rw_w0[l], rw_a0[l]])), _row(rw_k_k[l]), _row(rw_k_a[l]),
                            _row(rw_r_k[l]), _row(rw_ln_g[l]), _row(rw_ln_b[l]))

        o_f = o_gla + 2 * GLA_DK + D_BR
        w_gla = jnp.concatenate([wl[:, o_gla:o_f], wl[:, o_f + GLA_RANK:o_ret], wl[:, o_f:o_f + GLA_RANK],
                                 jnp.zeros((D_MODEL, LANES - GLA_RANK), BF16)], axis=1)
        f_up = jnp.concatenate([gla_f_up[l], jnp.zeros((LANES - GLA_RANK, GLA_DK), F32)], axis=0).astype(BF16)
        y_gla = _gla_branch(h, g_mix, w_gla, f_up, _row(gla_f_b[l]), _row(gla_norm_g[l]))

        y_ret = _ret_branch(h, pos, g_mix, wl[:, o_ret:o_lru], _row(ret_gn_g[l]))

        wax = jnp.concatenate([_dense_block_diag(lru_wa[l]), _dense_block_diag(lru_wx[l])], axis=1).astype(BF16)
        y_lru = _lru_branch(h, g_mix, wl[:, o_lru:o_mrg], lru_conv_w[l].astype(F32), _row(lru_conv_b[l]), wax,
                            _row(jnp.concatenate([lru_ba[l], lru_bx[l]])), _row(lru_lambda[l]))

        h = _merge(h, (y_rw, y_gla, y_ret, y_lru), g_mix, wl[:, o_mrg:], w_branch[l].astype(BF16),
                   w_out[l].astype(BF16))
        h = _xattn(h, mem, _row(xa_norm_g[l]), _row(xa_mem_norm_g[l]), xa_wq[l].astype(BF16),
                   xa_wkv[l].astype(BF16), xa_wo[l].astype(BF16), _row(final_norm_g), l == depth - 1)
    return h
```

```python
import functools

import numpy as np
import jax
import jax.numpy as jnp
from jax import lax
from jax.experimental import pallas as pl
from jax.experimental.pallas import tpu as pltpu

F32 = jnp.float32
BF16 = jnp.bfloat16

D_MODEL = 1024
D_BR = 512
N_BRANCH = 4
RW_HEADS = 8
RW_HD = D_BR // RW_HEADS
RW_RANK = 64
RW_LN_EPS = 64e-5
RW_SHIFT = 3 * D_BR + 2 * RW_RANK
GLA_DK = D_BR // 2
GLA_RANK = 16
GLA_TAU = 16.0
RET_DK = D_BR // 2
ROPE_BASE = 10000.0
ROPE_HALF = 32
LRU_BLOCKS = 8
LRU_CONV = 4
LRU_C = 8.0
XA_HEADS = 4
XA_HD = D_MODEL // XA_HEADS
NORM_EPS = 1e-6
CHUNK = 64
LANES = 128
CARRY_ROWS = 8
MAX_TIME_TILE = 512
VMEM_LIMIT_BYTES = 56 * 1024 * 1024


def _mm(a, b):
    return jnp.dot(a.astype(BF16), b.astype(BF16), preferred_element_type=F32)


def _mm_nt(a, b):
    return lax.dot_general(a.astype(BF16), b.astype(BF16), (((1,), (1,)), ((), ())),
                           preferred_element_type=F32)


def _mm_tn(a, b):
    return lax.dot_general(a.astype(BF16), b.astype(BF16), (((0,), (0,)), ((), ())),
                           preferred_element_type=F32)


def _split3(x):
    hi = x.astype(BF16)
    r = x - hi.astype(F32)
    mid = r.astype(BF16)
    lo = (r - mid.astype(F32)).astype(BF16)
    return hi, mid, lo


def _mm_sel(sel_bf16, x):
    hi, mid, lo = _split3(x)
    return _mm(sel_bf16, hi) + (_mm(sel_bf16, mid) + _mm(sel_bf16, lo))


def _rmsnorm(x, g):
    ms = jnp.mean(x * x, axis=-1, keepdims=True)
    return x * lax.rsqrt(ms + NORM_EPS) * g


def _sigmoid(x):
    return 1.0 / (1.0 + jnp.exp(-x))


def _softplus(x):
    return jnp.maximum(x, 0.0) + jnp.log(1.0 + jnp.exp(-jnp.abs(x)))


def _silu(x):
    return x * _sigmoid(x)


def _iota(shape, dim):
    return lax.broadcasted_iota(jnp.int32, shape, dim)


def _seg_sum(x, width):
    pieces = []
    for j in range(x.shape[-1] // LANES):
        xs = x[:, j * LANES:(j + 1) * LANES]
        if width == LANES:
            s = jnp.sum(xs, axis=-1, keepdims=True)
            pieces.append(jnp.broadcast_to(s, xs.shape))
        else:
            low = _iota(xs.shape, 1) < width
            s_lo = jnp.sum(jnp.where(low, xs, 0.0), axis=-1, keepdims=True)
            s_hi = jnp.sum(jnp.where(low, 0.0, xs), axis=-1, keepdims=True)
            pieces.append(jnp.where(low, s_lo, s_hi))
    return jnp.concatenate(pieces, axis=-1)


def _block_diag(x, n, row_blk, col_blk):
    t = jnp.concatenate([x] * n, axis=0)
    keep = (_iota(t.shape, 0) // row_blk) == (_iota(t.shape, 1) // col_blk)
    return jnp.where(keep, t, jnp.zeros_like(t))


def _block_mask(shape, row_blk, col_blk):
    return (_iota(shape, 0) // row_blk) == (_iota(shape, 1) // col_blk)


def _chunk_selectors(ts):
    r = _iota((ts, ts), 0)
    c = _iota((ts, ts), 1)
    same = (r // CHUNK) == (c // CHUNK)
    tri = jnp.where(same & (c <= r), 1.0, 0.0).astype(BF16)
    ones = jnp.where(same, 1.0, 0.0).astype(BF16)
    return tri, ones


def _const_spec(shape):
    nd = len(shape)
    return pl.BlockSpec(shape, lambda *_: (0,) * nd, pipeline_mode=pl.Buffered(1))


def _time_tile(s):
    ts = min(MAX_TIME_TILE, s)
    assert s % ts == 0 and ts % CHUNK == 0
    return ts


def _compiler_params(n_axes):
    return pltpu.CompilerParams(dimension_semantics=("arbitrary",) * n_axes,
                                vmem_limit_bytes=VMEM_LIMIT_BYTES)


def _rwkv_kernel(h_ref, g_ref, w_ref, mu_ref, wlr_ref, w0a0_ref, kk_ref, ka_ref, rk_ref, lng_ref, lnb_ref,
                 o_ref, zbuf, ht_ref, kt_ref, rt_ref, kd_ref, bd_ref, ke_ref, be_ref, v_ref, dec_ref,
                 wm_ref, u0_ref, akv_ref, ab_ref, y_ref):
    ts = h_ref.shape[0]
    hw = 4 * RW_HD

    @pl.when(pl.program_id(1) == 0)
    def _():
        zbuf[0:CARRY_ROWS, :] = jnp.zeros((CARRY_ROWS, RW_SHIFT), F32)
        ht_ref[...] = jnp.zeros(ht_ref.shape, F32)

    un = _rmsnorm(h_ref[...], g_ref[...])
    z = _mm(un, w_ref[...])
    zs = z[:, :RW_SHIFT]
    gate = z[:, RW_SHIFT:]
    zbuf[CARRY_ROWS:CARRY_ROWS + ts, :] = zs
    prev = zbuf[CARRY_ROWS - 1:CARRY_ROWS - 1 + ts, :]
    zbuf[0:CARRY_ROWS, :] = zbuf[ts:ts + CARRY_ROWS, :]
    sh = zs + (prev - zs) * mu_ref[...]
    r = sh[:, 0:D_BR]
    k = sh[:, D_BR:2 * D_BR]
    v = sh[:, 2 * D_BR:3 * D_BR]
    lr = sh[:, 3 * D_BR:]
    lr = jnp.where(_iota(lr.shape, 1) < RW_RANK, jnp.tanh(lr), lr)
    pre = _mm(lr, wlr_ref[...]) + w0a0_ref[...]
    log_w = -_softplus(-pre[:, :D_BR]) - 0.5
    lw = -jnp.exp(log_w)
    a = _sigmoid(pre[:, D_BR:])
    kk = k * kk_ref[...]
    kk = kk * lax.rsqrt(jnp.maximum(_seg_sum(kk * kk, RW_HD), 1e-24))
    k_mod = k * (1.0 + (a - 1.0) * ka_ref[...])
    b = kk * a
    bonus = _seg_sum(r * k_mod * rk_ref[...], RW_HD) * v

    tri, ones = _chunk_selectors(ts)
    c = _mm_sel(tri, lw)
    c_end = _mm_sel(ones, lw)
    inv = jnp.exp(-c)
    to_end = jnp.exp(c_end - c)
    kt_ref[...] = kk * jnp.exp(c - lw)
    rt_ref[...] = r * jnp.exp(c)
    kd_ref[...] = k_mod * inv
    bd_ref[...] = b * inv
    ke_ref[...] = k_mod * to_end
    be_ref[...] = -(b * to_end)
    v_ref[...] = v
    dec_ref[...] = jnp.exp(c_end)

    col = _iota((CHUNK, hw), 1) % CHUNK
    row = _iota((CHUNK, hw), 0)
    strict = col < row
    incl = col <= row
    eye = jnp.where(col == row, 1.0, 0.0)
    blk = _block_mask((hw, hw), RW_HD, RW_HD)
    bdg = functools.partial(_block_diag, n=4, row_blk=CHUNK, col_blk=RW_HD)
    groups = [slice(hw * gi, hw * (gi + 1)) for gi in range(RW_HEADS // 4)]

    inst = [(slice(ci * CHUNK, (ci + 1) * CHUNK), ls) for ci in range(ts // CHUNK) for ls in groups]
    kts = [kt_ref[rows, ls] for rows, ls in inst]
    lhs = [jnp.concatenate([kt, rt_ref[rows, ls]], axis=0) for kt, (rows, ls) in zip(kts, inst)]
    sbs = [_mm_nt(x, bdg(bd_ref[rows, ls])) for x, (rows, ls) in zip(lhs, inst)]
    sks = [_mm_nt(x, bdg(kd_ref[rows, ls])) for x, (rows, ls) in zip(lhs, inst)]
    for sb, (rows, ls) in zip(sbs, inst):
        ab_ref[rows, ls] = jnp.where(incl, sb[CHUNK:], 0.0)
    lk_ak = [jnp.concatenate([jnp.where(strict, sk[:CHUNK], 0.0), jnp.where(incl, sk[CHUNK:], 0.0)], axis=0)
             for sk in sks]
    ms = [jnp.where(strict, -sb[:CHUNK], 0.0) for sb in sbs]
    ps = [eye + m for m in ms]
    ms = [_mm(m, bdg(m)) for m in ms]
    for _ in range(4):
        sq = [_mm(jnp.concatenate([m, p], axis=0), bdg(m)) for m, p in zip(ms, ps)]
        ps = [p + x[CHUNK:] for p, x in zip(ps, sq)]
        ms = [x[:CHUNK] for x in sq]
    ps = [p + _mm(p, bdg(m)) for m, p in zip(ms, ps)]
    xs = [_mm(x, bdg(v_ref[rows, ls])) for x, (rows, ls) in zip(lk_ak, inst)]
    for p, kt, x, (rows, ls) in zip(ps, kts, xs, inst):
        akv_ref[rows, ls] = x[CHUNK:]
        wm_ref[rows, ls] = _mm(p, bdg(kt))
        u0_ref[rows, ls] = _mm(p, bdg(x[:CHUNK]))

    for ci in range(ts // CHUNK):
        r0 = ci * CHUNK
        rows = slice(r0, r0 + CHUNK)
        hts = [ht_ref[gi] for gi in range(len(groups))]
        wrs = [_mm_nt(jnp.concatenate([wm_ref[rows, ls], rt_ref[rows, ls]], axis=0), ht) for ht, ls in zip(hts, groups)]
        us = [wr[:CHUNK] + u0_ref[rows, ls] for wr, ls in zip(wrs, groups)]
        upds = [_mm_tn(jnp.concatenate([v_ref[rows, ls], u], axis=0),
                       jnp.concatenate([ke_ref[rows, ls], be_ref[rows, ls]], axis=0)) for u, ls in zip(us, groups)]
        for gi, (ht, upd, ls) in enumerate(zip(hts, upds, groups)):
            ht_ref[gi] = ht * dec_ref[r0:r0 + 1, ls] + jnp.where(blk, upd, 0.0)
        for wr, u, ls in zip(wrs, us, groups):
            y_ref[rows, ls] = wr[CHUNK:] + (akv_ref[rows, ls] - _mm(ab_ref[rows, ls], bdg(u)))

    y = y_ref[...]
    mean = _seg_sum(y, RW_HD) * (1.0 / RW_HD)
    yc = y - mean
    var = _seg_sum(yc * yc, RW_HD) * (1.0 / RW_HD)
    y = yc * lax.rsqrt(var + RW_LN_EPS) * lng_ref[...] + lnb_ref[...]
    o_ref[...] = ((y + bonus) * _silu(gate)).astype(o_ref.dtype)


def _rwkv_branch(h, g, w, mu, wlr, w0a0, k_k, k_a, r_k, ln_g, ln_b):
    bsz, s, _ = h.shape
    ts = _time_tile(s)
    n_in = w.shape[1]
    vec = lambda n: _const_spec((1, n))
    big = pltpu.VMEM((ts, D_BR), F32)
    return pl.pallas_call(
        _rwkv_kernel,
        grid=(bsz, s // ts),
        in_specs=[pl.BlockSpec((None, ts, D_MODEL), lambda b, t: (b, t, 0)),
                  vec(D_MODEL), _const_spec((D_MODEL, n_in)), vec(RW_SHIFT),
                  _const_spec((2 * RW_RANK, 2 * D_BR)), vec(2 * D_BR),
                  vec(D_BR), vec(D_BR), vec(D_BR), vec(D_BR), vec(D_BR)],
        out_specs=pl.BlockSpec((None, ts, D_BR), lambda b, t: (b, t, 0)),
        out_shape=jax.ShapeDtypeStruct((bsz, s, D_BR), BF16),
        scratch_shapes=[pltpu.VMEM((ts + CARRY_ROWS, RW_SHIFT), F32),
                        pltpu.VMEM((RW_HEADS // 4, 4 * RW_HD, 4 * RW_HD), F32),
                        big, big, big, big, big, big, big, big, big, big, big, big, big],
        compiler_params=_compiler_params(2),
        name="rwkv_branch",
    )(h, g, w, mu, wlr, w0a0, k_k, k_a, r_k, ln_g, ln_b)


def _gla_kernel(h_ref, g_ref, w_ref, fup_ref, fb_ref, ng_ref, o_ref, st_ref, qd_ref, kd_ref, ke_ref, v_ref, dec_ref, y_ref):
    ts = h_ref.shape[0]
    dk = GLA_DK // 4
    dv = D_BR // 4

    @pl.when(pl.program_id(1) == 0)
    def _():
        st_ref[...] = jnp.zeros(st_ref.shape, F32)

    un = _rmsnorm(h_ref[...], g_ref[...])
    z = _mm(un, w_ref[...])
    q = z[:, 0:GLA_DK]
    k = z[:, GLA_DK:2 * GLA_DK]
    v = z[:, 2 * GLA_DK:2 * GLA_DK + D_BR]
    gate = z[:, 2 * GLA_DK + D_BR:2 * GLA_DK + 2 * D_BR]
    f_lo = z[:, 2 * GLA_DK + 2 * D_BR:]
    f = _mm(f_lo, fup_ref[...]) + fb_ref[...]
    log_f = -_softplus(-f) * (1.0 / GLA_TAU)
    tri, ones = _chunk_selectors(ts)
    bcum = _mm_sel(tri, log_f)
    b_end = _mm_sel(ones, log_f)
    qd_ref[...] = q * (dk ** -0.5) * jnp.exp(bcum)
    kd_ref[...] = k * jnp.exp(-bcum)
    ke_ref[...] = k * jnp.exp(b_end - bcum)
    v_ref[...] = v
    dec_ref[...] = jnp.exp(b_end)

    causal = (_iota((CHUNK, 4 * CHUNK), 1) % CHUNK) <= _iota((CHUNK, 4 * CHUNK), 0)
    st_mask = _block_mask((D_BR, GLA_DK), dv, dk)

    chunks = [slice(ci * CHUNK, (ci + 1) * CHUNK) for ci in range(ts // CHUNK)]
    atts = [jnp.where(causal, _mm_nt(qd_ref[rows, :], _block_diag(kd_ref[rows, :], 4, CHUNK, dk)), 0.0)
            for rows in chunks]
    upds = [_mm_tn(v_ref[rows, :], ke_ref[rows, :]) for rows in chunks]
    st = st_ref[...]
    for rows, att, upd in zip(chunks, atts, upds):
        y_ref[rows, :] = _mm(att, _block_diag(v_ref[rows, :], 4, CHUNK, dv)) + _mm_nt(qd_ref[rows, :], st)
        st = st * dec_ref[rows.start:rows.start + 1, :] + jnp.where(st_mask, upd, 0.0)
    st_ref[...] = st

    o = y_ref[...]
    o = o * lax.rsqrt(_seg_sum(o * o, dv) * (1.0 / dv) + NORM_EPS)
    o_ref[...] = (o * ng_ref[...] * _silu(gate)).astype(o_ref.dtype)


def _gla_branch(h, g, w, f_up, f_b, norm_g):
    bsz, s, _ = h.shape
    ts = _time_tile(s)
    vec = lambda n: _const_spec((1, n))
    return pl.pallas_call(
        _gla_kernel,
        grid=(bsz, s // ts),
        in_specs=[pl.BlockSpec((None, ts, D_MODEL), lambda b, t: (b, t, 0)),
                  vec(D_MODEL), _const_spec(w.shape), _const_spec(f_up.shape), vec(GLA_DK), vec(D_BR)],
        out_specs=pl.BlockSpec((None, ts, D_BR), lambda b, t: (b, t, 0)),
        out_shape=jax.ShapeDtypeStruct((bsz, s, D_BR), BF16),
        scratch_shapes=[pltpu.VMEM((D_BR, GLA_DK), F32),
                        pltpu.VMEM((ts, GLA_DK), F32), pltpu.VMEM((ts, GLA_DK), F32),
                        pltpu.VMEM((ts, GLA_DK), F32), pltpu.VMEM((ts, D_BR), F32),
                        pltpu.VMEM((ts, GLA_DK), F32), pltpu.VMEM((ts, D_BR), F32)],
        compiler_params=_compiler_params(2),
        name="gla_branch",
    )(h, g, w, f_up, f_b, norm_g)


def _ret_kernel(h_ref, pos_ref, g_ref, w_ref, gn_ref, freq_ref, dmat_ref, qw_ref, kw_ref, cdec_ref,
                o_ref, st_ref, q_ref, k_ref, ke_ref, v_ref, y_ref):
    ts = h_ref.shape[0]
    dk = RET_DK // 4
    dv = D_BR // 4

    @pl.when(pl.program_id(1) == 0)
    def _():
        st_ref[...] = jnp.zeros(st_ref.shape, F32)

    un = _rmsnorm(h_ref[...], g_ref[...])
    z = _mm(un, w_ref[...])
    q = z[:, 0:RET_DK]
    k = z[:, RET_DK:2 * RET_DK]
    v = z[:, 2 * RET_DK:2 * RET_DK + D_BR]
    gate = z[:, 2 * RET_DK + D_BR:]

    lane = _iota((ts, LANES), 1)
    ang = pos_ref[...].astype(F32) * freq_ref[...]
    first = (lane % (2 * ROPE_HALF)) < ROPE_HALF
    cos1 = jnp.cos(ang)
    sin1 = jnp.where(first, -jnp.sin(ang), jnp.sin(ang))
    cos2 = jnp.concatenate([cos1, cos1], axis=-1)
    sin2 = jnp.concatenate([sin1, sin1], axis=-1)
    first2 = jnp.concatenate([first, first], axis=-1)

    def rope(t):
        partner = jnp.where(first2, pltpu.roll(t, RET_DK - ROPE_HALF, 1), pltpu.roll(t, ROPE_HALF, 1))
        return t * cos2 + partner * sin2

    q_ref[...] = rope(q)
    kr = rope(k) * (dk ** -0.5)
    k_ref[...] = kr
    ke_ref[...] = kr * jnp.concatenate([kw_ref[...]] * (ts // CHUNK), axis=0)
    v_ref[...] = v
    st_mask = _block_mask((D_BR, RET_DK), dv, dk)

    chunks = [slice(ci * CHUNK, (ci + 1) * CHUNK) for ci in range(ts // CHUNK)]
    atts = [_mm_nt(q_ref[rows, :], _block_diag(k_ref[rows, :], 4, CHUNK, dk)) * dmat_ref[...] for rows in chunks]
    upds = [_mm_tn(v_ref[rows, :], ke_ref[rows, :]) for rows in chunks]
    st = st_ref[...]
    for rows, att, upd in zip(chunks, atts, upds):
        y_ref[rows, :] = (_mm(att, _block_diag(v_ref[rows, :], 4, CHUNK, dv))
                          + _mm_nt(q_ref[rows, :], st) * qw_ref[...])
        st = st * cdec_ref[...] + jnp.where(st_mask, upd, 0.0)
    st_ref[...] = st

    o = y_ref[...]
    mean = _seg_sum(o, dv) * (1.0 / dv)
    oc = o - mean
    var = _seg_sum(oc * oc, dv) * (1.0 / dv)
    o = oc * lax.rsqrt(var + NORM_EPS)
    o_ref[...] = (o * gn_ref[...] * _silu(gate)).astype(o_ref.dtype)


def _ret_tables():
    h = 4
    dk = RET_DK // h
    dv = D_BR // h
    log_g = jnp.log(1.0 - jnp.exp2(-5.0 - jnp.arange(h, dtype=F32)))
    idx = jnp.arange(CHUNK, dtype=F32)
    diff = idx[:, None] - idx[None, :]
    dmat = jnp.where(diff >= 0, jnp.exp(jnp.maximum(diff, 0.0)[None] * log_g[:, None, None]), 0.0)
    dmat = jnp.transpose(dmat, (1, 0, 2)).reshape(CHUNK, h * CHUNK)
    q_w = jnp.exp((idx + 1.0)[:, None] * log_g)
    k_w = jnp.exp((CHUNK - 1.0 - idx)[:, None] * log_g)
    cdec = jnp.exp(CHUNK * log_g)
    freq = ROPE_BASE ** (-jnp.arange(ROPE_HALF, dtype=F32) / ROPE_HALF)
    return (jnp.tile(freq, LANES // ROPE_HALF)[None, :], dmat, jnp.repeat(q_w, dv, axis=1),
            jnp.repeat(k_w, dk, axis=1), jnp.repeat(cdec, dk)[None, :])


def _ret_branch(h, pos, g, w, gn_g):
    bsz, s, _ = h.shape
    ts = _time_tile(s)
    freq, dmat, q_w, k_w, cdec = _ret_tables()
    vec = lambda n: _const_spec((1, n))
    return pl.pallas_call(
        _ret_kernel,
        grid=(bsz, s // ts),
        in_specs=[pl.BlockSpec((None, ts, D_MODEL), lambda b, t: (b, t, 0)),
                  pl.BlockSpec((None, ts, 1), lambda b, t: (b, t, 0)),
                  vec(D_MODEL), _const_spec(w.shape), vec(D_BR), vec(LANES),
                  _const_spec(dmat.shape), _const_spec(q_w.shape), _const_spec(k_w.shape), _const_spec(cdec.shape)],
        out_specs=pl.BlockSpec((None, ts, D_BR), lambda b, t: (b, t, 0)),
        out_shape=jax.ShapeDtypeStruct((bsz, s, D_BR), BF16),
        scratch_shapes=[pltpu.VMEM((D_BR, RET_DK), F32),
                        pltpu.VMEM((ts, RET_DK), F32), pltpu.VMEM((ts, RET_DK), F32),
                        pltpu.VMEM((ts, RET_DK), F32), pltpu.VMEM((ts, D_BR), F32),
                        pltpu.VMEM((ts, D_BR), F32)],
        compiler_params=_compiler_params(2),
        name="ret_branch",
    )(h, pos, g, w, gn_g, freq, dmat, q_w, k_w, cdec)


def _lru_kernel(h_ref, g_ref, w_ref, cw_ref, cb_ref, wax_ref, bax_ref, lam_ref, o_ref, xbuf, hc_ref):
    ts = h_ref.shape[0]

    @pl.when(pl.program_id(1) == 0)
    def _():
        xbuf[0:CARRY_ROWS, :] = jnp.zeros((CARRY_ROWS, D_BR), F32)
        hc_ref[...] = jnp.zeros(hc_ref.shape, F32)

    un = _rmsnorm(h_ref[...], g_ref[...])
    z = _mm(un, w_ref[...])
    gate = z[:, D_BR:]
    xbuf[CARRY_ROWS:CARRY_ROWS + ts, :] = z[:, :D_BR]
    xc = cb_ref[...] + jnp.zeros((ts, D_BR), F32)
    for j in range(LRU_CONV):
        off = CARRY_ROWS - (LRU_CONV - 1) + j
        xc = xc + xbuf[off:off + ts, :] * cw_ref[j:j + 1, :]
    xbuf[0:CARRY_ROWS, :] = xbuf[ts:ts + CARRY_ROWS, :]

    ri = _sigmoid(_mm(xc, wax_ref[...]) + bax_ref[...])
    log_a = -LRU_C * ri[:, :D_BR] * _softplus(-lam_ref[...])
    a = jnp.exp(log_a)
    u = jnp.sqrt(1.0 - jnp.exp(2.0 * log_a)) * (ri[:, D_BR:] * xc)

    row = _iota((ts, D_BR), 0)
    d = 1
    while d < ts:
        keep = row >= d
        u = jnp.where(keep, a * pltpu.roll(u, d, 0) + u, u)
        a = jnp.where(keep, a * pltpu.roll(a, d, 0), a)
        d *= 2
    hh = u + a * hc_ref[0:1, :]
    hc_ref[...] = jnp.broadcast_to(hh[ts - 1:ts, :], hc_ref.shape)
    o_ref[...] = (hh * _silu(gate)).astype(o_ref.dtype)


def _lru_branch(h, g, w, conv_w, conv_b, wax, bax, lam):
    bsz, s, _ = h.shape
    ts = _time_tile(s)
    vec = lambda n: _const_spec((1, n))
    return pl.pallas_call(
        _lru_kernel,
        grid=(bsz, s // ts),
        in_specs=[pl.BlockSpec((None, ts, D_MODEL), lambda b, t: (b, t, 0)),
                  vec(D_MODEL), _const_spec(w.shape), _const_spec(conv_w.shape), vec(D_BR),
                  _const_spec(wax.shape), vec(2 * D_BR), vec(D_BR)],
        out_specs=pl.BlockSpec((None, ts, D_BR), lambda b, t: (b, t, 0)),
        out_shape=jax.ShapeDtypeStruct((bsz, s, D_BR), BF16),
        scratch_shapes=[pltpu.VMEM((ts + CARRY_ROWS, D_BR), F32), pltpu.VMEM((CARRY_ROWS, D_BR), F32)],
        compiler_params=_compiler_params(2),
        name="lru_branch",
    )(h, g, w, conv_w, conv_b, wax, bax, lam)


def _merge_kernel(h_ref, y0_ref, y1_ref, y2_ref, y3_ref, g_ref, wm_ref, wb_ref, wo_ref, o_ref):
    x = h_ref[...]
    un = _rmsnorm(x, g_ref[...]).astype(BF16)
    merged = None
    for n, y_ref in enumerate((y0_ref, y1_ref, y2_ref, y3_ref)):
        gate = _sigmoid(_mm(un, wm_ref[:, n * D_MODEL:(n + 1) * D_MODEL]))
        term = gate * _mm(y_ref[...], wb_ref[n])
        merged = term if merged is None else merged + term
    o_ref[...] = x + _mm(merged, wo_ref[...])


def _merge(h, ys, g, w_merge, w_branch, w_out):
    bsz, s, _ = h.shape
    ts = _time_tile(s)
    vec = lambda n: _const_spec((1, n))
    tile = lambda n: pl.BlockSpec((None, ts, n), lambda b, t: (b, t, 0))
    return pl.pallas_call(
        _merge_kernel,
        grid=(bsz, s // ts),
        in_specs=[tile(D_MODEL), tile(D_BR), tile(D_BR), tile(D_BR), tile(D_BR),
                  vec(D_MODEL), _const_spec(w_merge.shape), _const_spec(w_branch.shape), _const_spec(w_out.shape)],
        out_specs=tile(D_MODEL),
        out_shape=jax.ShapeDtypeStruct(h.shape, F32),
        compiler_params=_compiler_params(2),
        name="merge_out",
    )(h, *ys, g, w_merge, w_branch, w_out)


def _xattn_kernel(h_ref, mem_ref, g_ref, gm_ref, wq_ref, wkv_ref, wo_ref, gf_ref, o_ref, k_ref, v_ref, *, final_norm):
    @pl.when(pl.program_id(1) == 0)
    def _():
        mn = _rmsnorm(mem_ref[...], gm_ref[...])
        kv = _mm(mn, wkv_ref[...])
        k_ref[...] = kv[:, :D_MODEL].astype(BF16)
        v_ref[...] = kv[:, D_MODEL:].astype(BF16)

    x = h_ref[...]
    q = _mm(_rmsnorm(x, g_ref[...]), wq_ref[...])
    outs = []
    for hd in range(XA_HEADS):
        ls = slice(hd * XA_HD, (hd + 1) * XA_HD)
        sc = _mm_nt(q[:, ls], k_ref[:, ls]) * (XA_HD ** -0.5)
        sc = sc - jnp.max(sc, axis=-1, keepdims=True)
        e = jnp.exp(sc)
        pr = e / jnp.sum(e, axis=-1, keepdims=True)
        outs.append(_mm(pr, v_ref[:, ls]))
    out = x + _mm(jnp.concatenate(outs, axis=-1), wo_ref[...])
    if final_norm:
        out = _rmsnorm(out, gf_ref[...])
    o_ref[...] = out


def _xattn(h, mem, g, g_mem, wq, wkv, wo, g_final, final_norm):
    bsz, s, _ = h.shape
    n_mem = mem.shape[1]
    ts = _time_tile(s)
    vec = lambda n: _const_spec((1, n))
    tile = pl.BlockSpec((None, ts, D_MODEL), lambda b, t: (b, t, 0))
    return pl.pallas_call(
        functools.partial(_xattn_kernel, final_norm=final_norm),
        grid=(bsz, s // ts),
        in_specs=[tile, pl.BlockSpec((None, n_mem, D_MODEL), lambda b, t: (b, 0, 0)),
                  vec(D_MODEL), vec(D_MODEL), _const_spec(wq.shape), _const_spec(wkv.shape), _const_spec(wo.shape),
                  vec(D_MODEL)],
        out_specs=tile,
        out_shape=jax.ShapeDtypeStruct(h.shape, F32),
        scratch_shapes=[pltpu.VMEM((n_mem, D_MODEL), BF16), pltpu.VMEM((n_mem, D_MODEL), BF16)],
        compiler_params=_compiler_params(2),
        name="xattn_final" if final_norm else "xattn",
    )(h, mem, g, g_mem, wq, wkv, wo, g_final)


def _row(v):
    return v.reshape(1, -1).astype(F32)


def _dense_block_diag(w):
    n, c, d = w.shape
    eye = jnp.eye(n, dtype=w.dtype)
    return (eye[:, None, :, None] * w[:, :, None, :]).reshape(n * c, n * d)


def kernel(x, mem, positions, mix_norm_g, w_in, rw_mu, rw_w0, rw_w2, rw_a0, rw_a2, rw_k_k, rw_k_a, rw_r_k, rw_ln_g, rw_ln_b, gla_f_up, gla_f_b, gla_norm_g, ret_gn_g, lru_conv_w, lru_conv_b, lru_wa, lru_ba, lru_wx, lru_bx, lru_lambda, w_branch, w_out, xa_norm_g, xa_mem_norm_g, xa_wq, xa_wkv, xa_wo, final_norm_g):
    depth = w_in.shape[0]
    pos = positions[..., None]
    n_rw = RW_SHIFT + D_BR
    n_gla = 2 * GLA_DK + 2 * D_BR + GLA_RANK
    n_ret = 2 * RET_DK + 2 * D_BR
    o_gla = n_rw
    o_ret = o_gla + n_gla
    o_lru = o_ret + n_ret
    o_mrg = o_lru + 2 * D_BR
    h = x
    for l in range(depth):
        wl = w_in[l].astype(BF16)
        g_mix = _row(mix_norm_g[l])

        z64 = jnp.zeros((RW_RANK, D_BR), F32)
        wlr = jnp.concatenate([jnp.concatenate([rw_w2[l], z64], axis=1),
                               jnp.concatenate([z64, rw_a2[l]], axis=1)], axis=0).astype(BF16)
        y_rw = _rwkv_branch(h, g_mix, wl[:, :n_rw], _row(rw_mu[l]), wlr,
                            _row(jnp.concatenate([rw_w0[l], rw_a0[l]])), _row(rw_k_k[l]), _row(rw_k_a[l]),
                            _row(rw_r_k[l]), _row(rw_ln_g[l]), _row(rw_ln_b[l]))

        o_f = o_gla + 2 * GLA_DK + D_BR
        w_gla = jnp.concatenate([wl[:, o_gla:o_f], wl[:, o_f + GLA_RANK:o_ret], wl[:, o_f:o_f + GLA_RANK],
                                 jnp.zeros((D_MODEL, LANES - GLA_RANK), BF16)], axis=1)
        f_up = jnp.concatenate([gla_f_up[l], jnp.zeros((LANES - GLA_RANK, GLA_DK), F32)], axis=0).astype(BF16)
        y_gla = _gla_branch(h, g_mix, w_gla, f_up, _row(gla_f_b[l]), _row(gla_norm_g[l]))

        y_ret = _ret_branch(h, pos, g_mix, wl[:, o_ret:o_lru], _row(ret_gn_g[l]))

        wax = jnp.concatenate([_dense_block_diag(lru_wa[l]), _dense_block_diag(lru_wx[l])], axis=1).astype(BF16)
        y_lru = _lru_branch(h, g_mix, wl[:, o_lru:o_mrg], lru_conv_w[l].astype(F32), _row(lru_conv_b[l]), wax,
                            _row(jnp.concatenate([lru_ba[l], lru_bx[l]])), _row(lru_lambda[l]))

        h = _merge(h, (y_rw, y_gla, y_ret, y_lru), g_mix, wl[:, o_mrg:], w_branch[l].astype(BF16),
                   w_out[l].astype(BF16))
        h = _xattn(h, mem, _row(xa_norm_g[l]), _row(xa_mem_norm_g[l]), xa_wq[l].astype(BF16),
                   xa_wkv[l].astype(BF16), xa_wo[l].astype(BF16), _row(final_norm_g), l == depth - 1)
    return h
```

```python
import functools

import jax
import jax.numpy as jnp
from jax import lax
from jax.experimental import pallas as pl
from jax.experimental.pallas import tpu as pltpu

F32 = jnp.float32
BF16 = jnp.bfloat16

D_MODEL = 1024
D_BR = 512
RW_HEADS = 8
RW_HD = D_BR // RW_HEADS
RW_RANK = 64
RW_LN_EPS = 64e-5
RW_SHIFT = 3 * D_BR + 2 * RW_RANK
GLA_DK = D_BR // 2
GLA_RANK = 16
GLA_TAU = 16.0
RET_DK = D_BR // 2
ROPE_BASE = 10000.0
ROPE_HALF = 32
LRU_CONV = 4
LRU_C = 8.0
XA_HEADS = 4
XA_HD = D_MODEL // XA_HEADS
NORM_EPS = 1e-6
CHUNK = 64
LANES = 128
CARRY_ROWS = 8
MAX_TIME_TILE = 512
VMEM_LIMIT_BYTES = 56 * 1024 * 1024


def _mm(a, b):
    return jnp.dot(a.astype(BF16), b.astype(BF16), preferred_element_type=F32)


def _mm_nt(a, b):
    return lax.dot_general(a.astype(BF16), b.astype(BF16), (((1,), (1,)), ((), ())),
                           preferred_element_type=F32)


def _mm_tn(a, b):
    return lax.dot_general(a.astype(BF16), b.astype(BF16), (((0,), (0,)), ((), ())),
                           preferred_element_type=F32)


def _split2(x):
    hi = x.astype(BF16)
    lo = (x - hi.astype(F32)).astype(BF16)
    return hi, lo


def _rmsnorm(x, g):
    ms = jnp.mean(x * x, axis=-1, keepdims=True)
    return x * lax.rsqrt(ms + NORM_EPS) * g


def _sigmoid(x):
    return 1.0 / (1.0 + jnp.exp(-x))


def _softplus(x):
    return jnp.maximum(x, 0.0) + jnp.log(1.0 + jnp.exp(-jnp.abs(x)))


def _silu(x):
    return x * _sigmoid(x)


def _iota(shape, dim):
    return lax.broadcasted_iota(jnp.int32, shape, dim)


def _seg_sum(x, width):
    pieces = []
    for j in range(x.shape[-1] // LANES):
        xs = x[:, j * LANES:(j + 1) * LANES]
        if width == LANES:
            s = jnp.sum(xs, axis=-1, keepdims=True)
            pieces.append(jnp.broadcast_to(s, xs.shape))
        else:
            low = _iota(xs.shape, 1) < width
            s_lo = jnp.sum(jnp.where(low, xs, 0.0), axis=-1, keepdims=True)
            s_hi = jnp.sum(jnp.where(low, 0.0, xs), axis=-1, keepdims=True)
            pieces.append(jnp.where(low, s_lo, s_hi))
    return jnp.concatenate(pieces, axis=-1)


def _block_diag(x, n, row_blk, col_blk):
    t = jnp.concatenate([x] * n, axis=0)
    keep = (_iota(t.shape, 0) // row_blk) == (_iota(t.shape, 1) // col_blk)
    return jnp.where(keep, t, jnp.zeros_like(t))


def _block_mask(shape, row_blk, col_blk):
    return (_iota(shape, 0) // row_blk) == (_iota(shape, 1) // col_blk)


def _chunk_cumsum(x):
    ts = x.shape[0]
    tri = jnp.where(_iota((CHUNK, CHUNK), 1) <= _iota((CHUNK, CHUNK), 0), 1.0, 0.0).astype(BF16)
    hi, lo = _split2(x)
    cums = []
    for r0 in range(0, ts, CHUNK):
        cums.append(_mm(tri, hi[r0:r0 + CHUNK]) + _mm(tri, lo[r0:r0 + CHUNK]))
    ends = jnp.concatenate([c[CHUNK - 1:CHUNK] for c in cums], axis=0)
    return cums, ends


def _const_spec(shape):
    nd = len(shape)
    return pl.BlockSpec(shape, lambda *_: (0,) * nd, pipeline_mode=pl.Buffered(1))


def _time_tile(s):
    ts = min(MAX_TIME_TILE, s)
    assert s % ts == 0 and ts % CHUNK == 0
    return ts


def _compiler_params(n_axes):
    return pltpu.CompilerParams(dimension_semantics=("arbitrary",) * n_axes,
                                vmem_limit_bytes=VMEM_LIMIT_BYTES)


def _rwkv_kernel(un_ref, w_ref, mu_ref, wlr_ref, w0a0_ref, kk_ref, ka_ref, rk_ref, lng_ref, lnb_ref,
                 o_ref, zbuf, ht_ref, kt_ref, rt_ref, kd_ref, bd_ref, ke_ref, be_ref, v_ref, dec_ref,
                 wm_ref, u0_ref, akv_ref, ab_ref, y_ref):
    ts = un_ref.shape[0]
    hw = 4 * RW_HD

    @pl.when(pl.program_id(1) == 0)
    def _():
        zbuf[0:CARRY_ROWS, :] = jnp.zeros((CARRY_ROWS, RW_SHIFT), F32)
        ht_ref[...] = jnp.zeros(ht_ref.shape, F32)

    z = _mm(un_ref[...], w_ref[...])
    zs = z[:, :RW_SHIFT]
    gate = z[:, RW_SHIFT:]
    zbuf[CARRY_ROWS:CARRY_ROWS + ts, :] = zs
    prev = zbuf[CARRY_ROWS - 1:CARRY_ROWS - 1 + ts, :]
    zbuf[0:CARRY_ROWS, :] = zbuf[ts:ts + CARRY_ROWS, :]
    sh = zs + (prev - zs) * mu_ref[...]
    r = sh[:, 0:D_BR]
    k = sh[:, D_BR:2 * D_BR]
    v = sh[:, 2 * D_BR:3 * D_BR]
    lr = sh[:, 3 * D_BR:]
    lr = jnp.where(_iota(lr.shape, 1) < RW_RANK, jnp.tanh(lr), lr)
    pre = _mm(lr, wlr_ref[...]) + w0a0_ref[...]
    log_w = -_softplus(-pre[:, :D_BR]) - 0.5
    lw = -jnp.exp(log_w)
    a = _sigmoid(pre[:, D_BR:])
    kk = k * kk_ref[...]
    kk = kk * lax.rsqrt(jnp.maximum(_seg_sum(kk * kk, RW_HD), 1e-24))
    k_mod = k * (1.0 + (a - 1.0) * ka_ref[...])
    b = kk * a
    bonus = _seg_sum(r * k_mod * rk_ref[...], RW_HD) * v

    cums, ends = _chunk_cumsum(lw)
    c = jnp.concatenate(cums, axis=0)
    c_end = jnp.concatenate([jnp.broadcast_to(ends[i:i + 1], (CHUNK, D_BR)) for i in range(ts // CHUNK)], axis=0)
    inv = jnp.exp(-c)
    to_end = jnp.exp(c_end - c)
    kt_ref[...] = kk * jnp.exp(c - lw)
    rt_ref[...] = r * jnp.exp(c)
    kd_ref[...] = k_mod * inv
    bd_ref[...] = b * inv
    ke_ref[...] = k_mod * to_end
    be_ref[...] = -(b * to_end)
    v_ref[...] = v
    dec_ref[...] = jnp.exp(ends)

    col = _iota((CHUNK, hw), 1) % CHUNK
    row = _iota((CHUNK, hw), 0)
    strict = col < row
    incl = col <= row
    eye = jnp.where(col == row, 1.0, 0.0)
    blk = _block_mask((hw, hw), RW_HD, RW_HD)
    bdg = functools.partial(_block_diag, n=4, row_blk=CHUNK, col_blk=RW_HD)
    groups = [slice(hw * gi, hw * (gi + 1)) for gi in range(RW_HEADS // 4)]

    inst = [(slice(ci * CHUNK, (ci + 1) * CHUNK), ls) for ci in range(ts // CHUNK) for ls in groups]
    kts = [kt_ref[rows, ls] for rows, ls in inst]
    lhs = [jnp.concatenate([kt, rt_ref[rows, ls]], axis=0) for kt, (rows, ls) in zip(kts, inst)]
    sbs = [_mm_nt(x, bdg(bd_ref[rows, ls])) for x, (rows, ls) in zip(lhs, inst)]
    sks = [_mm_nt(x, bdg(kd_ref[rows, ls])) for x, (rows, ls) in zip(lhs, inst)]
    for sb, (rows, ls) in zip(sbs, inst):
        ab_ref[rows, ls] = jnp.where(incl, sb[CHUNK:], 0.0)
    lk_ak = [jnp.concatenate([jnp.where(strict, sk[:CHUNK], 0.0), jnp.where(incl, sk[CHUNK:], 0.0)], axis=0)
             for sk in sks]
    ms = [jnp.where(strict, -sb[:CHUNK], 0.0) for sb in sbs]
    ps = [eye + m for m in ms]
    ms = [_mm(m, bdg(m)) for m in ms]
    for _ in range(4):
        sq = [_mm(jnp.concatenate([m, p], axis=0), bdg(m)) for m, p in zip(ms, ps)]
        ps = [p + x[CHUNK:] for p, x in zip(ps, sq)]
        ms = [x[:CHUNK] for x in sq]
    ps = [p + _mm(p, bdg(m)) for m, p in zip(ms, ps)]
    xs = [_mm(x, bdg(v_ref[rows, ls])) for x, (rows, ls) in zip(lk_ak, inst)]
    for p, kt, x, (rows, ls) in zip(ps, kts, xs, inst):
        akv_ref[rows, ls] = x[CHUNK:]
        wm_ref[rows, ls] = _mm(p, bdg(kt))
        u0_ref[rows, ls] = _mm(p, bdg(x[:CHUNK]))

    for ci in range(ts // CHUNK):
        r0 = ci * CHUNK
        rows = slice(r0, r0 + CHUNK)
        hts = [ht_ref[gi] for gi in range(len(groups))]
        wrs = [_mm_nt(jnp.concatenate([wm_ref[rows, ls], rt_ref[rows, ls]], axis=0), ht) for ht, ls in zip(hts, groups)]
        us = [wr[:CHUNK] + u0_ref[rows, ls] for wr, ls in zip(wrs, groups)]
        upds = [_mm_tn(jnp.concatenate([v_ref[rows, ls], u], axis=0),
                       jnp.concatenate([ke_ref[rows, ls], be_ref[rows, ls]], axis=0)) for u, ls in zip(us, groups)]
        for gi, (ht, upd, ls) in enumerate(zip(hts, upds, groups)):
            ht_ref[gi] = ht * dec_ref[ci:ci + 1, ls] + jnp.where(blk, upd, 0.0)
        for wr, u, ls in zip(wrs, us, groups):
            y_ref[rows, ls] = wr[CHUNK:] + (akv_ref[rows, ls] - _mm(ab_ref[rows, ls], bdg(u)))

    y = y_ref[...]
    mean = _seg_sum(y, RW_HD) * (1.0 / RW_HD)
    yc = y - mean
    var = _seg_sum(yc * yc, RW_HD) * (1.0 / RW_HD)
    y = yc * lax.rsqrt(var + RW_LN_EPS) * lng_ref[...] + lnb_ref[...]
    o_ref[...] = ((y + bonus) * _silu(gate)).astype(o_ref.dtype)


def _rwkv_branch(un, w, mu, wlr, w0a0, k_k, k_a, r_k, ln_g, ln_b):
    bsz, s, _ = un.shape
    ts = _time_tile(s)
    n_in = w.shape[1]
    vec = lambda n: _const_spec((1, n))
    big = pltpu.VMEM((ts, D_BR), F32)
    return pl.pallas_call(
        _rwkv_kernel,
        grid=(bsz, s // ts),
        in_specs=[pl.BlockSpec((None, ts, D_MODEL), lambda b, t: (b, t, 0)),
                  _const_spec((D_MODEL, n_in)), vec(RW_SHIFT),
                  _const_spec((2 * RW_RANK, 2 * D_BR)), vec(2 * D_BR),
                  vec(D_BR), vec(D_BR), vec(D_BR), vec(D_BR), vec(D_BR)],
        out_specs=pl.BlockSpec((None, ts, D_BR), lambda b, t: (b, t, 0)),
        out_shape=jax.ShapeDtypeStruct((bsz, s, D_BR), BF16),
        scratch_shapes=[pltpu.VMEM((ts + CARRY_ROWS, RW_SHIFT), F32),
                        pltpu.VMEM((RW_HEADS // 4, 4 * RW_HD, 4 * RW_HD), F32),
                        big, big, big, big, big, big, big, pltpu.VMEM((ts // CHUNK, D_BR), F32),
                        big, big, big, big, big],
        compiler_params=_compiler_params(2),
        name="rwkv_branch",
    )(un, w, mu, wlr, w0a0, k_k, k_a, r_k, ln_g, ln_b)


def _gla_kernel(un_ref, w_ref, fup_ref, fb_ref, ng_ref, o_ref, st_ref, qd_ref, kd_ref, ke_ref, v_ref, dec_ref, y_ref):
    ts = un_ref.shape[0]
    dk = GLA_DK // 4
    dv = D_BR // 4

    @pl.when(pl.program_id(1) == 0)
    def _():
        st_ref[...] = jnp.zeros(st_ref.shape, F32)

    z = _mm(un_ref[...], w_ref[...])
    q = z[:, 0:GLA_DK]
    k = z[:, GLA_DK:2 * GLA_DK]
    v = z[:, 2 * GLA_DK:2 * GLA_DK + D_BR]
    gate = z[:, 2 * GLA_DK + D_BR:2 * GLA_DK + 2 * D_BR]
    f_lo = z[:, 2 * GLA_DK + 2 * D_BR:]
    f = _mm(f_lo, fup_ref[...]) + fb_ref[...]
    log_f = -_softplus(-f) * (1.0 / GLA_TAU)
    cums, ends = _chunk_cumsum(log_f)
    bcum = jnp.concatenate(cums, axis=0)
    b_end = jnp.concatenate([jnp.broadcast_to(ends[i:i + 1], (CHUNK, GLA_DK)) for i in range(ts // CHUNK)], axis=0)
    qd_ref[...] = q * (dk ** -0.5) * jnp.exp(bcum)
    kd_ref[...] = k * jnp.exp(-bcum)
    ke_ref[...] = k * jnp.exp(b_end - bcum)
    v_ref[...] = v
    dec_ref[...] = jnp.exp(ends)

    causal = (_iota((CHUNK, 4 * CHUNK), 1) % CHUNK) <= _iota((CHUNK, 4 * CHUNK), 0)
    st_mask = _block_mask((D_BR, GLA_DK), dv, dk)

    chunks = [slice(ci * CHUNK, (ci + 1) * CHUNK) for ci in range(ts // CHUNK)]
    atts = [jnp.where(causal, _mm_nt(qd_ref[rows, :], _block_diag(kd_ref[rows, :], 4, CHUNK, dk)), 0.0)
            for rows in chunks]
    upds = [_mm_tn(v_ref[rows, :], ke_ref[rows, :]) for rows in chunks]
    st = st_ref[...]
    for ci, (rows, att, upd) in enumerate(zip(chunks, atts, upds)):
        y_ref[rows, :] = _mm(att, _block_diag(v_ref[rows, :], 4, CHUNK, dv)) + _mm_nt(qd_ref[rows, :], st)
        st = st * dec_ref[ci:ci + 1, :] + jnp.where(st_mask, upd, 0.0)
    st_ref[...] = st

    o = y_ref[...]
    o = o * lax.rsqrt(_seg_sum(o * o, dv) * (1.0 / dv) + NORM_EPS)
    o_ref[...] = (o * ng_ref[...] * _silu(gate)).astype(o_ref.dtype)


def _gla_branch(un, w, f_up, f_b, norm_g):
    bsz, s, _ = un.shape
    ts = _time_tile(s)
    vec = lambda n: _const_spec((1, n))
    return pl.pallas_call(
        _gla_kernel,
        grid=(bsz, s // ts),
        in_specs=[pl.BlockSpec((None, ts, D_MODEL), lambda b, t: (b, t, 0)),
                  _const_spec(w.shape), _const_spec(f_up.shape), vec(GLA_DK), vec(D_BR)],
        out_specs=pl.BlockSpec((None, ts, D_BR), lambda b, t: (b, t, 0)),
        out_shape=jax.ShapeDtypeStruct((bsz, s, D_BR), BF16),
        scratch_shapes=[pltpu.VMEM((D_BR, GLA_DK), F32),
                        pltpu.VMEM((ts, GLA_DK), F32), pltpu.VMEM((ts, GLA_DK), F32),
                        pltpu.VMEM((ts, GLA_DK), F32), pltpu.VMEM((ts, D_BR), F32),
                        pltpu.VMEM((ts // CHUNK, GLA_DK), F32), pltpu.VMEM((ts, D_BR), F32)],
        compiler_params=_compiler_params(2),
        name="gla_branch",
    )(un, w, f_up, f_b, norm_g)


def _ret_kernel(un_ref, pos_ref, w_ref, gn_ref, freq_ref, dmat_ref, qw_ref, kw_ref, cdec_ref,
                o_ref, st_ref, q_ref, k_ref, ke_ref, v_ref, y_ref):
    ts = un_ref.shape[0]
    dk = RET_DK // 4
    dv = D_BR // 4

    @pl.when(pl.program_id(1) == 0)
    def _():
        st_ref[...] = jnp.zeros(st_ref.shape, F32)

    z = _mm(un_ref[...], w_ref[...])
    q = z[:, 0:RET_DK]
    k = z[:, RET_DK:2 * RET_DK]
    v = z[:, 2 * RET_DK:2 * RET_DK + D_BR]
    gate = z[:, 2 * RET_DK + D_BR:]

    lane = _iota((ts, LANES), 1)
    ang = pos_ref[...].astype(F32) * freq_ref[...]
    first = (lane % (2 * ROPE_HALF)) < ROPE_HALF
    cos1 = jnp.cos(ang)
    sin1 = jnp.where(first, -jnp.sin(ang), jnp.sin(ang))
    cos2 = jnp.concatenate([cos1, cos1], axis=-1)
    sin2 = jnp.concatenate([sin1, sin1], axis=-1)
    first2 = jnp.concatenate([first, first], axis=-1)

    def rope(t):
        partner = jnp.where(first2, pltpu.roll(t, RET_DK - ROPE_HALF, 1), pltpu.roll(t, ROPE_HALF, 1))
        return t * cos2 + partner * sin2

    q_ref[...] = rope(q)
    kr = rope(k) * (dk ** -0.5)
    k_ref[...] = kr
    ke_ref[...] = kr * jnp.concatenate([kw_ref[...]] * (ts // CHUNK), axis=0)
    v_ref[...] = v
    st_mask = _block_mask((D_BR, RET_DK), dv, dk)

    chunks = [slice(ci * CHUNK, (ci + 1) * CHUNK) for ci in range(ts // CHUNK)]
    atts = [_mm_nt(q_ref[rows, :], _block_diag(k_ref[rows, :], 4, CHUNK, dk)) * dmat_ref[...] for rows in chunks]
    upds = [_mm_tn(v_ref[rows, :], ke_ref[rows, :]) for rows in chunks]
    st = st_ref[...]
    for rows, att, upd in zip(chunks, atts, upds):
        y_ref[rows, :] = (_mm(att, _block_diag(v_ref[rows, :], 4, CHUNK, dv))
                          + _mm_nt(q_ref[rows, :], st) * qw_ref[...])
        st = st * cdec_ref[...] + jnp.where(st_mask, upd, 0.0)
    st_ref[...] = st

    o = y_ref[...]
    mean = _seg_sum(o, dv) * (1.0 / dv)
    oc = o - mean
    var = _seg_sum(oc * oc, dv) * (1.0 / dv)
    o = oc * lax.rsqrt(var + NORM_EPS)
    o_ref[...] = (o * gn_ref[...] * _silu(gate)).astype(o_ref.dtype)


def _ret_tables():
    h = 4
    dk = RET_DK // h
    dv = D_BR // h
    log_g = jnp.log(1.0 - jnp.exp2(-5.0 - jnp.arange(h, dtype=F32)))
    idx = jnp.arange(CHUNK, dtype=F32)
    diff = idx[:, None] - idx[None, :]
    dmat = jnp.where(diff >= 0, jnp.exp(jnp.maximum(diff, 0.0)[None] * log_g[:, None, None]), 0.0)
    dmat = jnp.transpose(dmat, (1, 0, 2)).reshape(CHUNK, h * CHUNK)
    q_w = jnp.exp((idx + 1.0)[:, None] * log_g)
    k_w = jnp.exp((CHUNK - 1.0 - idx)[:, None] * log_g)
    cdec = jnp.exp(CHUNK * log_g)
    freq = ROPE_BASE ** (-jnp.arange(ROPE_HALF, dtype=F32) / ROPE_HALF)
    return (jnp.tile(freq, LANES // ROPE_HALF)[None, :], dmat, jnp.repeat(q_w, dv, axis=1),
            jnp.repeat(k_w, dk, axis=1), jnp.repeat(cdec, dk)[None, :])


def _ret_branch(un, pos, w, gn_g):
    bsz, s, _ = un.shape
    ts = _time_tile(s)
    freq, dmat, q_w, k_w, cdec = _ret_tables()
    vec = lambda n: _const_spec((1, n))
    return pl.pallas_call(
        _ret_kernel,
        grid=(bsz, s // ts),
        in_specs=[pl.BlockSpec((None, ts, D_MODEL), lambda b, t: (b, t, 0)),
                  pl.BlockSpec((None, ts, 1), lambda b, t: (b, t, 0)),
                  _const_spec(w.shape), vec(D_BR), vec(LANES),
                  _const_spec(dmat.shape), _const_spec(q_w.shape), _const_spec(k_w.shape), _const_spec(cdec.shape)],
        out_specs=pl.BlockSpec((None, ts, D_BR), lambda b, t: (b, t, 0)),
        out_shape=jax.ShapeDtypeStruct((bsz, s, D_BR), BF16),
        scratch_shapes=[pltpu.VMEM((D_BR, RET_DK), F32),
                        pltpu.VMEM((ts, RET_DK), F32), pltpu.VMEM((ts, RET_DK), F32),
                        pltpu.VMEM((ts, RET_DK), F32), pltpu.VMEM((ts, D_BR), F32),
                        pltpu.VMEM((ts, D_BR), F32)],
        compiler_params=_compiler_params(2),
        name="ret_branch",
    )(un, pos, w, gn_g, freq, dmat, q_w, k_w, cdec)


def _lru_kernel(un_ref, w_ref, cw_ref, cb_ref, wax_ref, bax_ref, lam_ref, o_ref, xbuf, hc_ref):
    ts = un_ref.shape[0]

    @pl.when(pl.program_id(1) == 0)
    def _():
        xbuf[0:CARRY_ROWS, :] = jnp.zeros((CARRY_ROWS, D_BR), F32)
        hc_ref[...] = jnp.zeros(hc_ref.shape, F32)

    z = _mm(un_ref[...], w_ref[...])
    gate = z[:, D_BR:]
    xbuf[CARRY_ROWS:CARRY_ROWS + ts, :] = z[:, :D_BR]
    xc = cb_ref[...] + jnp.zeros((ts, D_BR), F32)
    for j in range(LRU_CONV):
        off = CARRY_ROWS - (LRU_CONV - 1) + j
        xc = xc + xbuf[off:off + ts, :] * cw_ref[j:j + 1, :]
    xbuf[0:CARRY_ROWS, :] = xbuf[ts:ts + CARRY_ROWS, :]

    ri = _sigmoid(_mm(xc, wax_ref[...]) + bax_ref[...])
    log_a = -LRU_C * ri[:, :D_BR] * _softplus(-lam_ref[...])
    a = jnp.exp(log_a)
    u = jnp.sqrt(1.0 - jnp.exp(2.0 * log_a)) * (ri[:, D_BR:] * xc)

    sub = _iota((CARRY_ROWS, D_BR), 0)
    carry = hc_ref[0:1, :]
    blocks = []
    for r0 in range(0, ts, CARRY_ROWS):
        ab = a[r0:r0 + CARRY_ROWS]
        ub = u[r0:r0 + CARRY_ROWS]
        for d in (1, 2, 4):
            keep = sub >= d
            ub = jnp.where(keep, ab * pltpu.roll(ub, d, 0) + ub, ub)
            ab = jnp.where(keep, ab * pltpu.roll(ab, d, 0), ab)
        hb = ub + ab * carry
        carry = hb[CARRY_ROWS - 1:CARRY_ROWS, :]
        blocks.append(hb)
    hh = jnp.concatenate(blocks, axis=0)
    hc_ref[...] = jnp.broadcast_to(hh[ts - 1:ts, :], hc_ref.shape)
    o_ref[...] = (hh * _silu(gate)).astype(o_ref.dtype)


def _lru_branch(un, w, conv_w, conv_b, wax, bax, lam):
    bsz, s, _ = un.shape
    ts = _time_tile(s)
    vec = lambda n: _const_spec((1, n))
    return pl.pallas_call(
        _lru_kernel,
        grid=(bsz, s // ts),
        in_specs=[pl.BlockSpec((None, ts, D_MODEL), lambda b, t: (b, t, 0)),
                  _const_spec(w.shape), _const_spec(conv_w.shape), vec(D_BR),
                  _const_spec(wax.shape), vec(2 * D_BR), vec(D_BR)],
        out_specs=pl.BlockSpec((None, ts, D_BR), lambda b, t: (b, t, 0)),
        out_shape=jax.ShapeDtypeStruct((bsz, s, D_BR), BF16),
        scratch_shapes=[pltpu.VMEM((ts + CARRY_ROWS, D_BR), F32), pltpu.VMEM((CARRY_ROWS, D_BR), F32)],
        compiler_params=_compiler_params(2),
        name="lru_branch",
    )(un, w, conv_w, conv_b, wax, bax, lam)


def _norm_kernel(x_ref, g_ref, o_ref):
    o_ref[...] = _rmsnorm(x_ref[...], g_ref[...]).astype(o_ref.dtype)


def _norm(x, g):
    bsz, s, _ = x.shape
    ts = _time_tile(s)
    tile = pl.BlockSpec((None, ts, D_MODEL), lambda b, t: (b, t, 0))
    return pl.pallas_call(
        _norm_kernel,
        grid=(bsz, s // ts),
        in_specs=[tile, _const_spec((1, D_MODEL))],
        out_specs=tile,
        out_shape=jax.ShapeDtypeStruct(x.shape, BF16),
        compiler_params=_compiler_params(2),
        name="mix_norm",
    )(x, g)


def _merge_kernel(h_ref, un_ref, y0_ref, y1_ref, y2_ref, y3_ref, gx_ref, wm_ref, wb_ref, wo_ref, o_ref, xn_ref):
    un = un_ref[...]
    merged = None
    for n, y_ref in enumerate((y0_ref, y1_ref, y2_ref, y3_ref)):
        gate = _sigmoid(_mm(un, wm_ref[:, n * D_MODEL:(n + 1) * D_MODEL]))
        term = gate * _mm(y_ref[...], wb_ref[n])
        merged = term if merged is None else merged + term
    out = h_ref[...] + _mm(merged, wo_ref[...])
    o_ref[...] = out
    xn_ref[...] = _rmsnorm(out, gx_ref[...]).astype(xn_ref.dtype)


def _merge(h, un, ys, g_xa, w_merge, w_branch, w_out):
    bsz, s, _ = h.shape
    ts = _time_tile(s)
    tile = lambda n: pl.BlockSpec((None, ts, n), lambda b, t: (b, t, 0))
    return pl.pallas_call(
        _merge_kernel,
        grid=(bsz, s // ts),
        in_specs=[tile(D_MODEL), tile(D_MODEL), tile(D_BR), tile(D_BR), tile(D_BR), tile(D_BR),
                  _const_spec((1, D_MODEL)), _const_spec(w_merge.shape), _const_spec(w_branch.shape),
                  _const_spec(w_out.shape)],
        out_specs=[tile(D_MODEL), tile(D_MODEL)],
        out_shape=[jax.ShapeDtypeStruct(h.shape, F32), jax.ShapeDtypeStruct(h.shape, BF16)],
        compiler_params=_compiler_params(2),
        name="merge_out",
    )(h, un, *ys, g_xa, w_merge, w_branch, w_out)


def _xattn_kernel(h_ref, xn_ref, mem_ref, gm_ref, wq_ref, wkv_ref, wo_ref, gn_ref, *rest, final_norm):
    if final_norm:
        o_ref, k_ref, v_ref = rest
    else:
        o_ref, un_ref, k_ref, v_ref = rest

    @pl.when(pl.program_id(1) == 0)
    def _():
        mn = _rmsnorm(mem_ref[...], gm_ref[...])
        kv = _mm(mn, wkv_ref[...])
        k_ref[...] = kv[:, :D_MODEL].astype(BF16)
        v_ref[...] = kv[:, D_MODEL:].astype(BF16)

    q = _mm(xn_ref[...], wq_ref[...])
    outs = []
    for hd in range(XA_HEADS):
        ls = slice(hd * XA_HD, (hd + 1) * XA_HD)
        sc = _mm_nt(q[:, ls], k_ref[:, ls]) * (XA_HD ** -0.5)
        sc = sc - jnp.max(sc, axis=-1, keepdims=True)
        e = jnp.exp(sc)
        pr = e / jnp.sum(e, axis=-1, keepdims=True)
        outs.append(_mm(pr, v_ref[:, ls]))
    out = h_ref[...] + _mm(jnp.concatenate(outs, axis=-1), wo_ref[...])
    normed = _rmsnorm(out, gn_ref[...])
    if final_norm:
        o_ref[...] = normed
    else:
        o_ref[...] = out
        un_ref[...] = normed.astype(un_ref.dtype)


def _xattn(h, xn, mem, g_mem, wq, wkv, wo, g_next, final_norm):
    bsz, s, _ = h.shape
    n_mem = mem.shape[1]
    ts = _time_tile(s)
    vec = lambda n: _const_spec((1, n))
    tile = pl.BlockSpec((None, ts, D_MODEL), lambda b, t: (b, t, 0))
    out_f32 = jax.ShapeDtypeStruct(h.shape, F32)
    return pl.pallas_call(
        functools.partial(_xattn_kernel, final_norm=final_norm),
        grid=(bsz, s // ts),
        in_specs=[tile, tile, pl.BlockSpec((None, n_mem, D_MODEL), lambda b, t: (b, 0, 0)),
                  vec(D_MODEL), _const_spec(wq.shape), _const_spec(wkv.shape), _const_spec(wo.shape),
                  vec(D_MODEL)],
        out_specs=tile if final_norm else [tile, tile],
        out_shape=out_f32 if final_norm else [out_f32, jax.ShapeDtypeStruct(h.shape, BF16)],
        scratch_shapes=[pltpu.VMEM((n_mem, D_MODEL), BF16), pltpu.VMEM((n_mem, D_MODEL), BF16)],
        compiler_params=_compiler_params(2),
        name="xattn_final" if final_norm else "xattn",
    )(h, xn, mem, g_mem, wq, wkv, wo, g_next)


def _row(v):
    return v.reshape(1, -1).astype(F32)


def _dense_block_diag(w):
    n, c, d = w.shape
    eye = jnp.eye(n, dtype=w.dtype)
    return (eye[:, None, :, None] * w[:, :, None, :]).reshape(n * c, n * d)


def kernel(x, mem, positions, mix_norm_g, w_in, rw_mu, rw_w0, rw_w2, rw_a0, rw_a2, rw_k_k, rw_k_a, rw_r_k, rw_ln_g, rw_ln_b, gla_f_up, gla_f_b, gla_norm_g, ret_gn_g, lru_conv_w, lru_conv_b, lru_wa, lru_ba, lru_wx, lru_bx, lru_lambda, w_branch, w_out, xa_norm_g, xa_mem_norm_g, xa_wq, xa_wkv, xa_wo, final_norm_g):
    depth = w_in.shape[0]
    pos = positions[..., None]
    n_rw = RW_SHIFT + D_BR
    n_gla = 2 * GLA_DK + 2 * D_BR + GLA_RANK
    n_ret = 2 * RET_DK + 2 * D_BR
    o_gla = n_rw
    o_ret = o_gla + n_gla
    o_lru = o_ret + n_ret
    o_mrg = o_lru + 2 * D_BR
    h = x
    un = _norm(x, _row(mix_norm_g[0]))
    for l in range(depth):
        wl = w_in[l].astype(BF16)
        last = l == depth - 1

        z64 = jnp.zeros((RW_RANK, D_BR), F32)
        wlr = jnp.concatenate([jnp.concatenate([rw_w2[l], z64], axis=1),
                               jnp.concatenate([z64, rw_a2[l]], axis=1)], axis=0).astype(BF16)
        y_rw = _rwkv_branch(un, wl[:, :n_rw], _row(rw_mu[l]), wlr,
                            _row(jnp.concatenate([rw_w0[l], rw_a0[l]])), _row(rw_k_k[l]), _row(rw_k_a[l]),
                            _row(rw_r_k[l]), _row(rw_ln_g[l]), _row(rw_ln_b[l]))

        o_f = o_gla + 2 * GLA_DK + D_BR
        w_gla = jnp.concatenate([wl[:, o_gla:o_f], wl[:, o_f + GLA_RANK:o_ret], wl[:, o_f:o_f + GLA_RANK],
                                 jnp.zeros((D_MODEL, LANES - GLA_RANK), BF16)], axis=1)
        f_up = jnp.concatenate([gla_f_up[l], jnp.zeros((LANES - GLA_RANK, GLA_DK), F32)], axis=0).astype(BF16)
        y_gla = _gla_branch(un, w_gla, f_up, _row(gla_f_b[l]), _row(gla_norm_g[l]))

        y_ret = _ret_branch(un, pos, wl[:, o_ret:o_lru], _row(ret_gn_g[l]))

        wax = jnp.concatenate([_dense_block_diag(lru_wa[l]), _dense_block_diag(lru_wx[l])], axis=1).astype(BF16)
        y_lru = _lru_branch(un, wl[:, o_lru:o_mrg], lru_conv_w[l].astype(F32), _row(lru_conv_b[l]), wax,
                            _row(jnp.concatenate([lru_ba[l], lru_bx[l]])), _row(lru_lambda[l]))

        h, xn = _merge(h, un, (y_rw, y_gla, y_ret, y_lru), _row(xa_norm_g[l]), wl[:, o_mrg:],
                       w_branch[l].astype(BF16), w_out[l].astype(BF16))
        g_next = _row(final_norm_g if last else mix_norm_g[l + 1])
        res = _xattn(h, xn, mem, _row(xa_mem_norm_g[l]), xa_wq[l].astype(BF16), xa_wkv[l].astype(BF16),
                     xa_wo[l].astype(BF16), g_next, last)
        if last:
            return res
        h, un = res
```

```python
import functools

import jax
import jax.numpy as jnp
from jax import lax
from jax.experimental import pallas as pl
from jax.experimental.pallas import tpu as pltpu

F32 = jnp.float32
BF16 = jnp.bfloat16

D_MODEL = 1024
D_BR = 512
RW_HEADS = 8
RW_HD = D_BR // RW_HEADS
RW_RANK = 64
RW_LN_EPS = 64e-5
RW_SHIFT = 3 * D_BR + 2 * RW_RANK
GLA_DK = D_BR // 2
GLA_RANK = 16
GLA_TAU = 16.0
RET_DK = D_BR // 2
ROPE_BASE = 10000.0
ROPE_HALF = 32
LRU_CONV = 4
LRU_C = 8.0
XA_HEADS = 4
XA_HD = D_MODEL // XA_HEADS
NORM_EPS = 1e-6
CHUNK = 64
LANES = 128
CARRY_ROWS = 8
MAX_TIME_TILE = 512
NORM_TILE_SCALE = 4
VMEM_LIMIT_BYTES = 56 * 1024 * 1024


def _mm(a, b):
    return jnp.dot(a.astype(BF16), b.astype(BF16), preferred_element_type=F32)


def _mm_nt(a, b):
    return lax.dot_general(a.astype(BF16), b.astype(BF16), (((1,), (1,)), ((), ())),
                           preferred_element_type=F32)


def _mm_tn(a, b):
    return lax.dot_general(a.astype(BF16), b.astype(BF16), (((0,), (0,)), ((), ())),
                           preferred_element_type=F32)


def _split2(x):
    hi = x.astype(BF16)
    lo = (x - hi.astype(F32)).astype(BF16)
    return hi, lo


def _split3(x):
    hi = x.astype(BF16)
    r = x - hi.astype(F32)
    mid = r.astype(BF16)
    lo = (r - mid.astype(F32)).astype(BF16)
    return hi, mid, lo


def _rmsnorm(x, g):
    ms = jnp.mean(x * x, axis=-1, keepdims=True)
    return x * lax.rsqrt(ms + NORM_EPS) * g


def _sigmoid(x):
    return 1.0 / (1.0 + jnp.exp(-x))


def _softplus(x):
    return jnp.maximum(x, 0.0) + jnp.log(1.0 + jnp.exp(-jnp.abs(x)))


def _silu(x):
    return x * _sigmoid(x)


def _iota(shape, dim):
    return lax.broadcasted_iota(jnp.int32, shape, dim)


def _seg_sum(x, width):
    pieces = []
    for j in range(x.shape[-1] // LANES):
        xs = x[:, j * LANES:(j + 1) * LANES]
        if width == LANES:
            s = jnp.sum(xs, axis=-1, keepdims=True)
            pieces.append(jnp.broadcast_to(s, xs.shape))
        else:
            low = _iota(xs.shape, 1) < width
            s_lo = jnp.sum(jnp.where(low, xs, 0.0), axis=-1, keepdims=True)
            s_hi = jnp.sum(jnp.where(low, 0.0, xs), axis=-1, keepdims=True)
            pieces.append(jnp.where(low, s_lo, s_hi))
    return jnp.concatenate(pieces, axis=-1)


def _block_diag(x, n, row_blk, col_blk):
    t = jnp.concatenate([x] * n, axis=0)
    keep = (_iota(t.shape, 0) // row_blk) == (_iota(t.shape, 1) // col_blk)
    return jnp.where(keep, t, jnp.zeros_like(t))


def _block_mask(shape, row_blk, col_blk):
    return (_iota(shape, 0) // row_blk) == (_iota(shape, 1) // col_blk)


def _chunk_cumsum(x):
    ts = x.shape[0]
    tri = jnp.where(_iota((CHUNK, CHUNK), 1) <= _iota((CHUNK, CHUNK), 0), 1.0, 0.0).astype(BF16)
    hi, lo = _split2(x)
    cums = []
    for r0 in range(0, ts, CHUNK):
        cums.append(_mm(tri, hi[r0:r0 + CHUNK]) + _mm(tri, lo[r0:r0 + CHUNK]))
    ends = jnp.concatenate([c[CHUNK - 1:CHUNK] for c in cums], axis=0)
    return cums, ends


def _const_spec(shape):
    nd = len(shape)
    return pl.BlockSpec(shape, lambda *_: (0,) * nd, pipeline_mode=pl.Buffered(1))


def _time_tile(s):
    ts = min(MAX_TIME_TILE, s)
    assert s % ts == 0 and ts % CHUNK == 0
    return ts


def _compiler_params(n_axes):
    return pltpu.CompilerParams(dimension_semantics=("arbitrary",) * n_axes,
                                vmem_limit_bytes=VMEM_LIMIT_BYTES)


def _rwkv_kernel(un_ref, w_ref, mu_ref, wlr_ref, w0a0_ref, kk_ref, ka_ref, rk_ref, lng_ref, lnb_ref,
                 o_ref, zbuf, ht_ref, kt_ref, rt_ref, kd_ref, bd_ref, ke_ref, be_ref, v_ref, dec_ref,
                 wm_ref, u0_ref, akv_ref, ab_ref, y_ref):
    ts = un_ref.shape[0]
    hw = 4 * RW_HD

    @pl.when(pl.program_id(1) == 0)
    def _():
        zbuf[0:CARRY_ROWS, :] = jnp.zeros((CARRY_ROWS, RW_SHIFT), F32)
        ht_ref[...] = jnp.zeros(ht_ref.shape, F32)

    z = _mm(un_ref[...], w_ref[...])
    zs = z[:, :RW_SHIFT]
    gate = z[:, RW_SHIFT:]
    zbuf[CARRY_ROWS:CARRY_ROWS + ts, :] = zs
    prev = zbuf[CARRY_ROWS - 1:CARRY_ROWS - 1 + ts, :]
    zbuf[0:CARRY_ROWS, :] = zbuf[ts:ts + CARRY_ROWS, :]
    sh = zs + (prev - zs) * mu_ref[...]
    r = sh[:, 0:D_BR]
    k = sh[:, D_BR:2 * D_BR]
    v = sh[:, 2 * D_BR:3 * D_BR]
    lr = sh[:, 3 * D_BR:]
    lr = jnp.where(_iota(lr.shape, 1) < RW_RANK, jnp.tanh(lr), lr)
    pre = _mm(lr, wlr_ref[...]) + w0a0_ref[...]
    log_w = -_softplus(-pre[:, :D_BR]) - 0.5
    lw = -jnp.exp(log_w)
    a = _sigmoid(pre[:, D_BR:])
    kk = k * kk_ref[...]
    kk = kk * lax.rsqrt(jnp.maximum(_seg_sum(kk * kk, RW_HD), 1e-24))
    k_mod = k * (1.0 + (a - 1.0) * ka_ref[...])
    b = kk * a
    bonus = _seg_sum(r * k_mod * rk_ref[...], RW_HD) * v

    cums, ends = _chunk_cumsum(lw)
    c = jnp.concatenate(cums, axis=0)
    c_end = jnp.concatenate([jnp.broadcast_to(ends[i:i + 1], (CHUNK, D_BR)) for i in range(ts // CHUNK)], axis=0)
    inv = jnp.exp(-c)
    to_end = jnp.exp(c_end - c)
    kt_ref[...] = kk * jnp.exp(c - lw)
    rt_ref[...] = r * jnp.exp(c)
    kd_ref[...] = k_mod * inv
    bd_ref[...] = b * inv
    ke_ref[...] = k_mod * to_end
    be_ref[...] = -(b * to_end)
    v_ref[...] = v
    dec_ref[...] = jnp.exp(ends)

    col = _iota((CHUNK, hw), 1) % CHUNK
    row = _iota((CHUNK, hw), 0)
    strict = col < row
    incl = col <= row
    eye = jnp.where(col == row, 1.0, 0.0)
    blk = _block_mask((hw, hw), RW_HD, RW_HD)
    bdg = functools.partial(_block_diag, n=4, row_blk=CHUNK, col_blk=RW_HD)
    groups = [slice(hw * gi, hw * (gi + 1)) for gi in range(RW_HEADS // 4)]

    inst = [(slice(ci * CHUNK, (ci + 1) * CHUNK), ls) for ci in range(ts // CHUNK) for ls in groups]
    kts = [kt_ref[rows, ls] for rows, ls in inst]
    lhs = [jnp.concatenate([kt, rt_ref[rows, ls]], axis=0) for kt, (rows, ls) in zip(kts, inst)]
    sbs = [_mm_nt(x, bdg(bd_ref[rows, ls])) for x, (rows, ls) in zip(lhs, inst)]
    sks = [_mm_nt(x, bdg(kd_ref[rows, ls])) for x, (rows, ls) in zip(lhs, inst)]
    for sb, (rows, ls) in zip(sbs, inst):
        ab_ref[rows, ls] = jnp.where(incl, sb[CHUNK:], 0.0)
    lk_ak = [jnp.concatenate([jnp.where(strict, sk[:CHUNK], 0.0), jnp.where(incl, sk[CHUNK:], 0.0)], axis=0)
             for sk in sks]
    ms = [jnp.where(strict, -sb[:CHUNK], 0.0) for sb in sbs]
    ps = [eye + m for m in ms]
    ms = [_mm(m, bdg(m)) for m in ms]
    for _ in range(4):
        sq = [_mm(jnp.concatenate([m, p], axis=0), bdg(m)) for m, p in zip(ms, ps)]
        ps = [p + x[CHUNK:] for p, x in zip(ps, sq)]
        ms = [x[:CHUNK] for x in sq]
    ps = [p + _mm(p, bdg(m)) for m, p in zip(ms, ps)]
    xs = [_mm(x, bdg(v_ref[rows, ls])) for x, (rows, ls) in zip(lk_ak, inst)]
    for p, kt, x, (rows, ls) in zip(ps, kts, xs, inst):
        akv_ref[rows, ls] = x[CHUNK:]
        wm_ref[rows, ls] = _mm(p, bdg(kt))
        u0_ref[rows, ls] = _mm(p, bdg(x[:CHUNK]))

    for ci in range(ts // CHUNK):
        r0 = ci * CHUNK
        rows = slice(r0, r0 + CHUNK)
        hts = [ht_ref[gi] for gi in range(len(groups))]
        wrs = [_mm_nt(jnp.concatenate([wm_ref[rows, ls], rt_ref[rows, ls]], axis=0), ht) for ht, ls in zip(hts, groups)]
        us = [wr[:CHUNK] + u0_ref[rows, ls] for wr, ls in zip(wrs, groups)]
        upds = [_mm_tn(jnp.concatenate([v_ref[rows, ls], u], axis=0),
                       jnp.concatenate([ke_ref[rows, ls], be_ref[rows, ls]], axis=0)) for u, ls in zip(us, groups)]
        for gi, (ht, upd, ls) in enumerate(zip(hts, upds, groups)):
            ht_ref[gi] = ht * dec_ref[ci:ci + 1, ls] + jnp.where(blk, upd, 0.0)
        for wr, u, ls in zip(wrs, us, groups):
            y_ref[rows, ls] = wr[CHUNK:] + (akv_ref[rows, ls] - _mm(ab_ref[rows, ls], bdg(u)))

    y = y_ref[...]
    mean = _seg_sum(y, RW_HD) * (1.0 / RW_HD)
    yc = y - mean
    var = _seg_sum(yc * yc, RW_HD) * (1.0 / RW_HD)
    y = yc * lax.rsqrt(var + RW_LN_EPS) * lng_ref[...] + lnb_ref[...]
    o_ref[...] = ((y + bonus) * _silu(gate)).astype(o_ref.dtype)


def _rwkv_branch(un, w, mu, wlr, w0a0, k_k, k_a, r_k, ln_g, ln_b):
    bsz, s, _ = un.shape
    ts = _time_tile(s)
    n_in = w.shape[1]
    vec = lambda n: _const_spec((1, n))
    big = pltpu.VMEM((ts, D_BR), F32)
    return pl.pallas_call(
        _rwkv_kernel,
        grid=(bsz, s // ts),
        in_specs=[pl.BlockSpec((None, ts, D_MODEL), lambda b, t: (b, t, 0)),
                  _const_spec((D_MODEL, n_in)), vec(RW_SHIFT),
                  _const_spec((2 * RW_RANK, 2 * D_BR)), vec(2 * D_BR),
                  vec(D_BR), vec(D_BR), vec(D_BR), vec(D_BR), vec(D_BR)],
        out_specs=pl.BlockSpec((None, ts, D_BR), lambda b, t: (b, t, 0)),
        out_shape=jax.ShapeDtypeStruct((bsz, s, D_BR), BF16),
        scratch_shapes=[pltpu.VMEM((ts + CARRY_ROWS, RW_SHIFT), F32),
                        pltpu.VMEM((RW_HEADS // 4, 4 * RW_HD, 4 * RW_HD), F32),
                        big, big, big, big, big, big, big, pltpu.VMEM((ts // CHUNK, D_BR), F32),
                        big, big, big, big, big],
        compiler_params=_compiler_params(2),
        name="rwkv_branch",
    )(un, w, mu, wlr, w0a0, k_k, k_a, r_k, ln_g, ln_b)


def _interleave(stage_fns, n_parts):
    n_stages = len(stage_fns)
    for step in range(n_stages + n_parts - 1):
        live = [stage_fns[step - part](part) for part in range(n_parts) if 0 <= step - part < n_stages]
        while live:
            for gen in list(live):
                if next(gen, StopIteration) is StopIteration:
                    live.remove(gen)


def _gla_kernel(un_ref, w_ref, fup_ref, fb_ref, ng_ref, o_ref, st_ref, qd_ref, kd_ref, ke_ref, v_ref, dec_ref, y_ref):
    ts = un_ref.shape[0]
    dk = GLA_DK // 4
    dv = D_BR // 4
    n_parts = 2
    part_rows = ts // n_parts
    part_chunks = part_rows // CHUNK
    n_in = w_ref.shape[1]

    @pl.when(pl.program_id(1) == 0)
    def _():
        st_ref[...] = jnp.zeros(st_ref.shape, F32)

    causal = (_iota((CHUNK, 4 * CHUNK), 1) % CHUNK) <= _iota((CHUNK, 4 * CHUNK), 0)
    st_mask = _block_mask((D_BR, GLA_DK), dv, dk)
    zs, gates, atts, upds = {}, {}, {}, {}

    def rows_of(part):
        return slice(part * part_rows, (part + 1) * part_rows)

    def chunks_of(part):
        return [(part * part_chunks + i, slice(part * part_rows + i * CHUNK, part * part_rows + (i + 1) * CHUNK))
                for i in range(part_chunks)]

    def project(part):
        un = un_ref[rows_of(part), :]
        cols = []
        for c0 in range(0, n_in, 2 * LANES):
            cols.append(_mm(un, w_ref[:, c0:min(c0 + 2 * LANES, n_in)]))
            yield
        zs[part] = jnp.concatenate(cols, axis=-1)

    def decays(part):
        z = zs.pop(part)
        rows = rows_of(part)
        gates[part] = z[:, 2 * GLA_DK + D_BR:2 * GLA_DK + 2 * D_BR]
        f = _mm(z[:, 2 * GLA_DK + 2 * D_BR:], fup_ref[...]) + fb_ref[...]
        yield
        log_f = -_softplus(-f) * (1.0 / GLA_TAU)
        yield
        cums, ends = _chunk_cumsum(log_f)
        yield
        bcum = jnp.concatenate(cums, axis=0)
        b_end = jnp.concatenate([jnp.broadcast_to(ends[i:i + 1], (CHUNK, GLA_DK)) for i in range(part_chunks)], axis=0)
        qd_ref[rows, :] = z[:, 0:GLA_DK] * (dk ** -0.5) * jnp.exp(bcum)
        yield
        kd_ref[rows, :] = z[:, GLA_DK:2 * GLA_DK] * jnp.exp(-bcum)
        yield
        ke_ref[rows, :] = z[:, GLA_DK:2 * GLA_DK] * jnp.exp(b_end - bcum)
        yield
        v_ref[rows, :] = z[:, 2 * GLA_DK:2 * GLA_DK + D_BR]
        dec_ref[part * part_chunks:(part + 1) * part_chunks, :] = jnp.exp(ends)
        yield

    def scores(part):
        atts[part], upds[part] = [], []
        for _, rows in chunks_of(part):
            att = _mm_nt(qd_ref[rows, :], _block_diag(kd_ref[rows, :], 4, CHUNK, dk))
            atts[part].append(jnp.where(causal, att, 0.0))
            yield
            upds[part].append(_mm_tn(v_ref[rows, :], ke_ref[rows, :]))
            yield

    def states(part):
        st = st_ref[...]
        for (ci, rows), att, upd in zip(chunks_of(part), atts.pop(part), upds.pop(part)):
            y_ref[rows, :] = _mm(att, _block_diag(v_ref[rows, :], 4, CHUNK, dv)) + _mm_nt(qd_ref[rows, :], st)
            st = st * dec_ref[ci:ci + 1, :] + jnp.where(st_mask, upd, 0.0)
            yield
        st_ref[...] = st

    def finish(part):
        gate = gates.pop(part)
        for i, (_, rows) in enumerate(chunks_of(part)):
            o = y_ref[rows, :]
            o = o * lax.rsqrt(_seg_sum(o * o, dv) * (1.0 / dv) + NORM_EPS)
            o_ref[rows, :] = (o * ng_ref[...] * _silu(gate[i * CHUNK:(i + 1) * CHUNK])).astype(o_ref.dtype)
            yield

    _interleave([project, decays, scores, states, finish], n_parts)


def _gla_branch(un, w, f_up, f_b, norm_g):
    bsz, s, _ = un.shape
    ts = _time_tile(s)
    vec = lambda n: _const_spec((1, n))
    return pl.pallas_call(
        _gla_kernel,
        grid=(bsz, s // ts),
        in_specs=[pl.BlockSpec((None, ts, D_MODEL), lambda b, t: (b, t, 0)),
                  _const_spec(w.shape), _const_spec(f_up.shape), vec(GLA_DK), vec(D_BR)],
        out_specs=pl.BlockSpec((None, ts, D_BR), lambda b, t: (b, t, 0)),
        out_shape=jax.ShapeDtypeStruct((bsz, s, D_BR), BF16),
        scratch_shapes=[pltpu.VMEM((D_BR, GLA_DK), F32),
                        pltpu.VMEM((ts, GLA_DK), F32), pltpu.VMEM((ts, GLA_DK), F32),
                        pltpu.VMEM((ts, GLA_DK), F32), pltpu.VMEM((ts, D_BR), F32),
                        pltpu.VMEM((ts // CHUNK, GLA_DK), F32), pltpu.VMEM((ts, D_BR), F32)],
        compiler_params=_compiler_params(2),
        name="gla_branch",
    )(un, w, f_up, f_b, norm_g)


def _ret_kernel(un_ref, cos_ref, sin_ref, w_ref, gn_ref, dmat_ref, qw_ref, kw_ref, cdec_ref,
                o_ref, st_ref, q_ref, k_ref, ke_ref, v_ref, y_ref):
    ts = un_ref.shape[0]
    dk = RET_DK // 4
    dv = D_BR // 4

    @pl.when(pl.program_id(1) == 0)
    def _():
        st_ref[...] = jnp.zeros(st_ref.shape, F32)

    z = _mm(un_ref[...], w_ref[...])
    q = z[:, 0:RET_DK]
    k = z[:, RET_DK:2 * RET_DK]
    v = z[:, 2 * RET_DK:2 * RET_DK + D_BR]
    gate = z[:, 2 * RET_DK + D_BR:]

    cos2 = jnp.concatenate([cos_ref[...]] * (RET_DK // LANES), axis=-1)
    sin2 = jnp.concatenate([sin_ref[...]] * (RET_DK // LANES), axis=-1)
    first2 = (_iota((ts, RET_DK), 1) % (2 * ROPE_HALF)) < ROPE_HALF

    def rope(t):
        partner = jnp.where(first2, pltpu.roll(t, RET_DK - ROPE_HALF, 1), pltpu.roll(t, ROPE_HALF, 1))
        return t * cos2 + partner * sin2

    q_ref[...] = rope(q)
    kr = rope(k) * (dk ** -0.5)
    k_ref[...] = kr
    ke_ref[...] = kr * jnp.concatenate([kw_ref[...]] * (ts // CHUNK), axis=0)
    v_ref[...] = v
    st_mask = _block_mask((D_BR, RET_DK), dv, dk)

    chunks = [slice(ci * CHUNK, (ci + 1) * CHUNK) for ci in range(ts // CHUNK)]
    atts = [_mm_nt(q_ref[rows, :], _block_diag(k_ref[rows, :], 4, CHUNK, dk)) * dmat_ref[...] for rows in chunks]
    upds = [_mm_tn(v_ref[rows, :], ke_ref[rows, :]) for rows in chunks]
    st = st_ref[...]
    for rows, att, upd in zip(chunks, atts, upds):
        y_ref[rows, :] = (_mm(att, _block_diag(v_ref[rows, :], 4, CHUNK, dv))
                          + _mm_nt(q_ref[rows, :], st) * qw_ref[...])
        st = st * cdec_ref[...] + jnp.where(st_mask, upd, 0.0)
    st_ref[...] = st

    o = y_ref[...]
    mean = _seg_sum(o, dv) * (1.0 / dv)
    oc = o - mean
    var = _seg_sum(oc * oc, dv) * (1.0 / dv)
    o = oc * lax.rsqrt(var + NORM_EPS)
    o_ref[...] = (o * gn_ref[...] * _silu(gate)).astype(o_ref.dtype)


def _ret_tables():
    h = 4
    dk = RET_DK // h
    dv = D_BR // h
    log_g = jnp.log(1.0 - jnp.exp2(-5.0 - jnp.arange(h, dtype=F32)))
    idx = jnp.arange(CHUNK, dtype=F32)
    diff = idx[:, None] - idx[None, :]
    dmat = jnp.where(diff >= 0, jnp.exp(jnp.maximum(diff, 0.0)[None] * log_g[:, None, None]), 0.0)
    dmat = jnp.transpose(dmat, (1, 0, 2)).reshape(CHUNK, h * CHUNK)
    q_w = jnp.exp((idx + 1.0)[:, None] * log_g)
    k_w = jnp.exp((CHUNK - 1.0 - idx)[:, None] * log_g)
    cdec = jnp.exp(CHUNK * log_g)
    return dmat, jnp.repeat(q_w, dv, axis=1), jnp.repeat(k_w, dk, axis=1), jnp.repeat(cdec, dk)[None, :]


def _rope_kernel(pos_ref, freq_ref, cos_ref, sin_ref):
    ang = pos_ref[...].astype(F32) * freq_ref[...]
    cs = jnp.concatenate([jnp.cos(ang), jnp.sin(ang)], axis=0)
    er = _iota((2 * ROPE_HALF, 2 * LANES), 0)
    ec = _iota((2 * ROPE_HALF, 2 * LANES), 1)
    hit = (ec % ROPE_HALF) == (er % ROPE_HALF)
    sign = jnp.where((ec % (2 * ROPE_HALF)) < ROPE_HALF, -1.0, 1.0)
    spread = jnp.where(hit & (er < ROPE_HALF) & (ec < LANES), 1.0,
                       jnp.where(hit & (er >= ROPE_HALF) & (ec >= LANES), sign, 0.0)).astype(BF16)
    c_hi, c_mid, c_lo = _split3(cs)
    tab = _mm_tn(c_hi, spread) + (_mm_tn(c_mid, spread) + _mm_tn(c_lo, spread))
    cos_ref[...] = tab[:, :LANES]
    sin_ref[...] = tab[:, LANES:]


def _rope_tables(positions):
    bsz, s = positions.shape
    ts = _time_tile(s)
    freq = ROPE_BASE ** (-jnp.arange(ROPE_HALF, dtype=F32) / ROPE_HALF)
    freq = jnp.broadcast_to(freq[:, None], (ROPE_HALF, ts))
    tab = jax.ShapeDtypeStruct((bsz, s, LANES), F32)
    out_spec = pl.BlockSpec((None, ts, LANES), lambda b, t: (b, t, 0))
    return pl.pallas_call(
        _rope_kernel,
        grid=(bsz, s // ts),
        in_specs=[pl.BlockSpec((None, 1, ts), lambda b, t: (b, 0, t)), _const_spec((ROPE_HALF, ts))],
        out_specs=[out_spec, out_spec],
        out_shape=[tab, tab],
        compiler_params=_compiler_params(2),
        name="rope_tables",
    )(positions[:, None, :], freq)


def _ret_branch(un, cos, sin, w, gn_g):
    bsz, s, _ = un.shape
    ts = _time_tile(s)
    dmat, q_w, k_w, cdec = _ret_tables()
    tab = pl.BlockSpec((None, ts, LANES), lambda b, t: (b, t, 0))
    vec = lambda n: _const_spec((1, n))
    return pl.pallas_call(
        _ret_kernel,
        grid=(bsz, s // ts),
        in_specs=[pl.BlockSpec((None, ts, D_MODEL), lambda b, t: (b, t, 0)),
                  tab, tab, _const_spec(w.shape), vec(D_BR),
                  _const_spec(dmat.shape), _const_spec(q_w.shape), _const_spec(k_w.shape), _const_spec(cdec.shape)],
        out_specs=pl.BlockSpec((None, ts, D_BR), lambda b, t: (b, t, 0)),
        out_shape=jax.ShapeDtypeStruct((bsz, s, D_BR), BF16),
        scratch_shapes=[pltpu.VMEM((D_BR, RET_DK), F32),
                        pltpu.VMEM((ts, RET_DK), F32), pltpu.VMEM((ts, RET_DK), F32),
                        pltpu.VMEM((ts, RET_DK), F32), pltpu.VMEM((ts, D_BR), F32),
                        pltpu.VMEM((ts, D_BR), F32)],
        compiler_params=_compiler_params(2),
        name="ret_branch",
    )(un, cos, sin, w, gn_g, dmat, q_w, k_w, cdec)


def _lru_kernel(un_ref, w_ref, cw_ref, cb_ref, wax_ref, bax_ref, lam_ref, o_ref, xbuf, hc_ref):
    ts = un_ref.shape[0]

    @pl.when(pl.program_id(1) == 0)
    def _():
        xbuf[0:CARRY_ROWS, :] = jnp.zeros((CARRY_ROWS, D_BR), F32)
        hc_ref[...] = jnp.zeros(hc_ref.shape, F32)

    z = _mm(un_ref[...], w_ref[...])
    gate = z[:, D_BR:]
    xbuf[CARRY_ROWS:CARRY_ROWS + ts, :] = z[:, :D_BR]
    xc = cb_ref[...] + jnp.zeros((ts, D_BR), F32)
    for j in range(LRU_CONV):
        off = CARRY_ROWS - (LRU_CONV - 1) + j
        xc = xc + xbuf[off:off + ts, :] * cw_ref[j:j + 1, :]
    xbuf[0:CARRY_ROWS, :] = xbuf[ts:ts + CARRY_ROWS, :]

    ri = _sigmoid(_mm(xc, wax_ref[...]) + bax_ref[...])
    log_a = -LRU_C * ri[:, :D_BR] * _softplus(-lam_ref[...])
    a = jnp.exp(log_a)
    u = jnp.sqrt(1.0 - jnp.exp(2.0 * log_a)) * (ri[:, D_BR:] * xc)

    sub = _iota((CARRY_ROWS, D_BR), 0)
    carry = hc_ref[0:1, :]
    blocks = []
    for r0 in range(0, ts, CARRY_ROWS):
        ab = a[r0:r0 + CARRY_ROWS]
        ub = u[r0:r0 + CARRY_ROWS]
        for d in (1, 2, 4):
            keep = sub >= d
            ub = jnp.where(keep, ab * pltpu.roll(ub, d, 0) + ub, ub)
            ab = jnp.where(keep, ab * pltpu.roll(ab, d, 0), ab)
        hb = ub + ab * carry
        carry = hb[CARRY_ROWS - 1:CARRY_ROWS, :]
        blocks.append(hb)
    hh = jnp.concatenate(blocks, axis=0)
    hc_ref[...] = jnp.broadcast_to(hh[ts - 1:ts, :], hc_ref.shape)
    o_ref[...] = (hh * _silu(gate)).astype(o_ref.dtype)


def _lru_branch(un, w, conv_w, conv_b, wax, bax, lam):
    bsz, s, _ = un.shape
    ts = _time_tile(s)
    vec = lambda n: _const_spec((1, n))
    return pl.pallas_call(
        _lru_kernel,
        grid=(bsz, s // ts),
        in_specs=[pl.BlockSpec((None, ts, D_MODEL), lambda b, t: (b, t, 0)),
                  _const_spec(w.shape), _const_spec(conv_w.shape), vec(D_BR),
                  _const_spec(wax.shape), vec(2 * D_BR), vec(D_BR)],
        out_specs=pl.BlockSpec((None, ts, D_BR), lambda b, t: (b, t, 0)),
        out_shape=jax.ShapeDtypeStruct((bsz, s, D_BR), BF16),
        scratch_shapes=[pltpu.VMEM((ts + CARRY_ROWS, D_BR), F32), pltpu.VMEM((CARRY_ROWS, D_BR), F32)],
        compiler_params=_compiler_params(2),
        name="lru_branch",
    )(un, w, conv_w, conv_b, wax, bax, lam)


def _norm_kernel(x_ref, g_ref, o_ref):
    o_ref[...] = _rmsnorm(x_ref[...], g_ref[...]).astype(o_ref.dtype)


def _norm(x, g):
    bsz, s, _ = x.shape
    ts = _time_tile(s)
    ts = ts * (NORM_TILE_SCALE if s % (ts * NORM_TILE_SCALE) == 0 else 1)
    tile = pl.BlockSpec((None, ts, D_MODEL), lambda b, t: (b, t, 0))
    return pl.pallas_call(
        _norm_kernel,
        grid=(bsz, s // ts),
        in_specs=[tile, _const_spec((1, D_MODEL))],
        out_specs=tile,
        out_shape=jax.ShapeDtypeStruct(x.shape, BF16),
        compiler_params=_compiler_params(2),
        name="mix_norm",
    )(x, g)


def _merge_kernel(h_ref, un_ref, y0_ref, y1_ref, y2_ref, y3_ref, gx_ref, wm_ref, wb_ref, wo_ref, o_ref, xn_ref):
    un = un_ref[...]
    merged = None
    for n, y_ref in enumerate((y0_ref, y1_ref, y2_ref, y3_ref)):
        gate = _sigmoid(_mm(un, wm_ref[:, n * D_MODEL:(n + 1) * D_MODEL]))
        term = gate * _mm(y_ref[...], wb_ref[n])
        merged = term if merged is None else merged + term
    out = h_ref[...] + _mm(merged, wo_ref[...])
    o_ref[...] = out
    xn_ref[...] = _rmsnorm(out, gx_ref[...]).astype(xn_ref.dtype)


def _merge(h, un, ys, g_xa, w_merge, w_branch, w_out):
    bsz, s, _ = h.shape
    ts = _time_tile(s)
    tile = lambda n: pl.BlockSpec((None, ts, n), lambda b, t: (b, t, 0))
    return pl.pallas_call(
        _merge_kernel,
        grid=(bsz, s // ts),
        in_specs=[tile(D_MODEL), tile(D_MODEL), tile(D_BR), tile(D_BR), tile(D_BR), tile(D_BR),
                  _const_spec((1, D_MODEL)), _const_spec(w_merge.shape), _const_spec(w_branch.shape),
                  _const_spec(w_out.shape)],
        out_specs=[tile(D_MODEL), tile(D_MODEL)],
        out_shape=[jax.ShapeDtypeStruct(h.shape, F32), jax.ShapeDtypeStruct(h.shape, BF16)],
        compiler_params=_compiler_params(2),
        name="merge_out",
    )(h, un, *ys, g_xa, w_merge, w_branch, w_out)


def _lru_merge_kernel(h_ref, un_ref, y0_ref, y1_ref, y2_ref, gx_ref, wm_ref, wb_ref, wo_ref,
                      wl_ref, cw_ref, cb_ref, wax_ref, bax_ref, lam_ref, o_ref, xn_ref, xbuf, hc_ref):
    ts = un_ref.shape[0]

    @pl.when(pl.program_id(1) == 0)
    def _():
        xbuf[0:CARRY_ROWS, :] = jnp.zeros((CARRY_ROWS, D_BR), F32)
        hc_ref[...] = jnp.zeros(hc_ref.shape, F32)

    un = un_ref[...]

    def branch_term(n, y):
        gate = _sigmoid(_mm(un, wm_ref[:, n * D_MODEL:(n + 1) * D_MODEL]))
        return gate * _mm(y, wb_ref[n])

    z = _mm(un, wl_ref[...])
    merged = branch_term(0, y0_ref[...])
    gate = z[:, D_BR:]
    xbuf[CARRY_ROWS:CARRY_ROWS + ts, :] = z[:, :D_BR]
    xc = cb_ref[...] + jnp.zeros((ts, D_BR), F32)
    for j in range(LRU_CONV):
        off = CARRY_ROWS - (LRU_CONV - 1) + j
        xc = xc + xbuf[off:off + ts, :] * cw_ref[j:j + 1, :]
    xbuf[0:CARRY_ROWS, :] = xbuf[ts:ts + CARRY_ROWS, :]
    ri = _sigmoid(_mm(xc, wax_ref[...]) + bax_ref[...])
    merged = merged + branch_term(1, y1_ref[...])
    log_a = -LRU_C * ri[:, :D_BR] * _softplus(-lam_ref[...])
    a = jnp.exp(log_a)
    u = jnp.sqrt(1.0 - jnp.exp(2.0 * log_a)) * (ri[:, D_BR:] * xc)
    merged = merged + branch_term(2, y2_ref[...])
    gate3 = _sigmoid(_mm(un, wm_ref[:, 3 * D_MODEL:]))

    sub = _iota((CARRY_ROWS, D_BR), 0)
    carry = hc_ref[0:1, :]
    blocks = []
    for r0 in range(0, ts, CARRY_ROWS):
        ab = a[r0:r0 + CARRY_ROWS]
        ub = u[r0:r0 + CARRY_ROWS]
        for d in (1, 2, 4):
            keep = sub >= d
            ub = jnp.where(keep, ab * pltpu.roll(ub, d, 0) + ub, ub)
            ab = jnp.where(keep, ab * pltpu.roll(ab, d, 0), ab)
        hb = ub + ab * carry
        carry = hb[CARRY_ROWS - 1:CARRY_ROWS, :]
        blocks.append(hb)
    hh = jnp.concatenate(blocks, axis=0)
    hc_ref[...] = jnp.broadcast_to(hh[ts - 1:ts, :], hc_ref.shape)
    merged = merged + gate3 * _mm(hh * _silu(gate), wb_ref[3])
    out = h_ref[...] + _mm(merged, wo_ref[...])
    o_ref[...] = out
    xn_ref[...] = _rmsnorm(out, gx_ref[...]).astype(xn_ref.dtype)


def _lru_merge(h, un, ys, g_xa, w_merge, w_branch, w_out, w_lru, conv_w, conv_b, wax, bax, lam):
    bsz, s, _ = h.shape
    ts = _time_tile(s)
    vec = lambda n: _const_spec((1, n))
    tile = lambda n: pl.BlockSpec((None, ts, n), lambda b, t: (b, t, 0))
    return pl.pallas_call(
        _lru_merge_kernel,
        grid=(bsz, s // ts),
        in_specs=[tile(D_MODEL), tile(D_MODEL), tile(D_BR), tile(D_BR), tile(D_BR),
                  vec(D_MODEL), _const_spec(w_merge.shape), _const_spec(w_branch.shape), _const_spec(w_out.shape),
                  _const_spec(w_lru.shape), _const_spec(conv_w.shape), vec(D_BR), _const_spec(wax.shape),
                  vec(2 * D_BR), vec(D_BR)],
        out_specs=[tile(D_MODEL), tile(D_MODEL)],
        out_shape=[jax.ShapeDtypeStruct(h.shape, F32), jax.ShapeDtypeStruct(h.shape, BF16)],
        scratch_shapes=[pltpu.VMEM((ts + CARRY_ROWS, D_BR), F32), pltpu.VMEM((CARRY_ROWS, D_BR), F32)],
        compiler_params=_compiler_params(2),
        name="lru_merge_out",
    )(h, un, *ys, g_xa, w_merge, w_branch, w_out, w_lru, conv_w, conv_b, wax, bax, lam)


def _xattn_kernel(h_ref, xn_ref, mem_ref, gm_ref, wq_ref, wkv_ref, wo_ref, gn_ref, *rest, final_norm):
    if final_norm:
        o_ref, k_ref, v_ref = rest
    else:
        o_ref, un_ref, k_ref, v_ref = rest

    @pl.when(pl.program_id(1) == 0)
    def _():
        mn = _rmsnorm(mem_ref[...], gm_ref[...])
        kv = _mm(mn, wkv_ref[...])
        k_ref[...] = kv[:, :D_MODEL].astype(BF16)
        v_ref[...] = kv[:, D_MODEL:].astype(BF16)

    q = _mm(xn_ref[...], wq_ref[...])
    outs = []
    for hd in range(XA_HEADS):
        ls = slice(hd * XA_HD, (hd + 1) * XA_HD)
        sc = _mm_nt(q[:, ls], k_ref[:, ls]) * (XA_HD ** -0.5)
        sc = sc - jnp.max(sc, axis=-1, keepdims=True)
        e = jnp.exp(sc)
        pr = e / jnp.sum(e, axis=-1, keepdims=True)
        outs.append(_mm(pr, v_ref[:, ls]))
    out = h_ref[...] + _mm(jnp.concatenate(outs, axis=-1), wo_ref[...])
    normed = _rmsnorm(out, gn_ref[...])
    if final_norm:
        o_ref[...] = normed
    else:
        o_ref[...] = out
        un_ref[...] = normed.astype(un_ref.dtype)


def _xattn(h, xn, mem, g_mem, wq, wkv, wo, g_next, final_norm):
    bsz, s, _ = h.shape
    n_mem = mem.shape[1]
    ts = _time_tile(s)
    vec = lambda n: _const_spec((1, n))
    tile = pl.BlockSpec((None, ts, D_MODEL), lambda b, t: (b, t, 0))
    out_f32 = jax.ShapeDtypeStruct(h.shape, F32)
    return pl.pallas_call(
        functools.partial(_xattn_kernel, final_norm=final_norm),
        grid=(bsz, s // ts),
        in_specs=[tile, tile, pl.BlockSpec((None, n_mem, D_MODEL), lambda b, t: (b, 0, 0)),
                  vec(D_MODEL), _const_spec(wq.shape), _const_spec(wkv.shape), _const_spec(wo.shape),
                  vec(D_MODEL)],
        out_specs=tile if final_norm else [tile, tile],
        out_shape=out_f32 if final_norm else [out_f32, jax.ShapeDtypeStruct(h.shape, BF16)],
        scratch_shapes=[pltpu.VMEM((n_mem, D_MODEL), BF16), pltpu.VMEM((n_mem, D_MODEL), BF16)],
        compiler_params=_compiler_params(2),
        name="xattn_final" if final_norm else "xattn",
    )(h, xn, mem, g_mem, wq, wkv, wo, g_next)


def _row(v):
    return v.reshape(1, -1).astype(F32)


def _dense_block_diag(w):
    n, c, d = w.shape
    eye = jnp.eye(n, dtype=w.dtype)
    return (eye[:, None, :, None] * w[:, :, None, :]).reshape(n * c, n * d)


def kernel(x, mem, positions, mix_norm_g, w_in, rw_mu, rw_w0, rw_w2, rw_a0, rw_a2, rw_k_k, rw_k_a, rw_r_k, rw_ln_g, rw_ln_b, gla_f_up, gla_f_b, gla_norm_g, ret_gn_g, lru_conv_w, lru_conv_b, lru_wa, lru_ba, lru_wx, lru_bx, lru_lambda, w_branch, w_out, xa_norm_g, xa_mem_norm_g, xa_wq, xa_wkv, xa_wo, final_norm_g):
    depth = w_in.shape[0]
    cos, sin = _rope_tables(positions)
    n_rw = RW_SHIFT + D_BR
    n_gla = 2 * GLA_DK + 2 * D_BR + GLA_RANK
    n_ret = 2 * RET_DK + 2 * D_BR
    o_gla = n_rw
    o_ret = o_gla + n_gla
    o_lru = o_ret + n_ret
    o_mrg = o_lru + 2 * D_BR
    h = x
    un = _norm(x, _row(mix_norm_g[0]))
    for l in range(depth):
        wl = w_in[l].astype(BF16)
        last = l == depth - 1

        z64 = jnp.zeros((RW_RANK, D_BR), F32)
        wlr = jnp.concatenate([jnp.concatenate([rw_w2[l], z64], axis=1),
                               jnp.concatenate([z64, rw_a2[l]], axis=1)], axis=0).astype(BF16)
        y_rw = _rwkv_branch(un, wl[:, :n_rw], _row(rw_mu[l]), wlr,
                            _row(jnp.concatenate([rw_w0[l], rw_a0[l]])), _row(rw_k_k[l]), _row(rw_k_a[l]),
                            _row(rw_r_k[l]), _row(rw_ln_g[l]), _row(rw_ln_b[l]))

        o_f = o_gla + 2 * GLA_DK + D_BR
        w_gla = jnp.concatenate([wl[:, o_gla:o_f], wl[:, o_f + GLA_RANK:o_ret], wl[:, o_f:o_f + GLA_RANK],
                                 jnp.zeros((D_MODEL, LANES - GLA_RANK), BF16)], axis=1)
        f_up = jnp.concatenate([gla_f_up[l], jnp.zeros((LANES - GLA_RANK, GLA_DK), F32)], axis=0).astype(BF16)
        y_gla = _gla_branch(un, w_gla, f_up, _row(gla_f_b[l]), _row(gla_norm_g[l]))

        y_ret = _ret_branch(un, cos, sin, wl[:, o_ret:o_lru], _row(ret_gn_g[l]))

        wax = jnp.concatenate([_dense_block_diag(lru_wa[l]), _dense_block_diag(lru_wx[l])], axis=1).astype(BF16)
        h, xn = _lru_merge(h, un, (y_rw, y_gla, y_ret), _row(xa_norm_g[l]), wl[:, o_mrg:],
                           w_branch[l].astype(BF16), w_out[l].astype(BF16), wl[:, o_lru:o_mrg],
                           lru_conv_w[l].astype(F32), _row(lru_conv_b[l]), wax,
                           _row(jnp.concatenate([lru_ba[l], lru_bx[l]])), _row(lru_lambda[l]))
        g_next = _row(final_norm_g if last else mix_norm_g[l + 1])
        res = _xattn(h, xn, mem, _row(xa_mem_norm_g[l]), xa_wq[l].astype(BF16), xa_wkv[l].astype(BF16),
                     xa_wo[l].astype(BF16), g_next, last)
        if last:
            return res
        h, un = res
```

```python
import functools

import jax
import jax.numpy as jnp
from jax import lax
from jax.experimental import pallas as pl
from jax.experimental.pallas import tpu as pltpu

F32 = jnp.float32
BF16 = jnp.bfloat16

D_MODEL = 1024
D_BR = 512
RW_HEADS = 8
RW_HD = D_BR // RW_HEADS
RW_RANK = 64
RW_LN_EPS = 64e-5
RW_SHIFT = 3 * D_BR + 2 * RW_RANK
GLA_DK = D_BR // 2
GLA_RANK = 16
GLA_TAU = 16.0
RET_DK = D_BR // 2
ROPE_BASE = 10000.0
ROPE_HALF = 32
LRU_CONV = 4
LRU_C = 8.0
XA_HEADS = 4
XA_HD = D_MODEL // XA_HEADS
NORM_EPS = 1e-6
CHUNK = 64
LANES = 128
CARRY_ROWS = 8
MAX_TIME_TILE = 512
NORM_TILE_SCALE = 4
VMEM_LIMIT_BYTES = 56 * 1024 * 1024


def _mm(a, b):
    return jnp.dot(a.astype(BF16), b.astype(BF16), preferred_element_type=F32)


def _mm_nt(a, b):
    return lax.dot_general(a.astype(BF16), b.astype(BF16), (((1,), (1,)), ((), ())),
                           preferred_element_type=F32)


def _mm_tn(a, b):
    return lax.dot_general(a.astype(BF16), b.astype(BF16), (((0,), (0,)), ((), ())),
                           preferred_element_type=F32)


def _split2(x):
    hi = x.astype(BF16)
    lo = (x - hi.astype(F32)).astype(BF16)
    return hi, lo


def _split3(x):
    hi = x.astype(BF16)
    r = x - hi.astype(F32)
    mid = r.astype(BF16)
    lo = (r - mid.astype(F32)).astype(BF16)
    return hi, mid, lo


def _rmsnorm(x, g):
    ms = jnp.mean(x * x, axis=-1, keepdims=True)
    return x * lax.rsqrt(ms + NORM_EPS) * g


def _sigmoid(x):
    return 1.0 / (1.0 + jnp.exp(-x))


def _softplus(x):
    return jnp.maximum(x, 0.0) + jnp.log(1.0 + jnp.exp(-jnp.abs(x)))


def _silu(x):
    return x * _sigmoid(x)


def _iota(shape, dim):
    return lax.broadcasted_iota(jnp.int32, shape, dim)


def _seg_sum(x, width):
    pieces = []
    for j in range(x.shape[-1] // LANES):
        xs = x[:, j * LANES:(j + 1) * LANES]
        if width == LANES:
            s = jnp.sum(xs, axis=-1, keepdims=True)
            pieces.append(jnp.broadcast_to(s, xs.shape))
        else:
            low = _iota(xs.shape, 1) < width
            s_lo = jnp.sum(jnp.where(low, xs, 0.0), axis=-1, keepdims=True)
            s_hi = jnp.sum(jnp.where(low, 0.0, xs), axis=-1, keepdims=True)
            pieces.append(jnp.where(low, s_lo, s_hi))
    return jnp.concatenate(pieces, axis=-1)


def _block_diag(x, n, row_blk, col_blk):
    t = jnp.concatenate([x] * n, axis=0)
    keep = (_iota(t.shape, 0) // row_blk) == (_iota(t.shape, 1) // col_blk)
    return jnp.where(keep, t, jnp.zeros_like(t))


def _block_mask(shape, row_blk, col_blk):
    return (_iota(shape, 0) // row_blk) == (_iota(shape, 1) // col_blk)


def _chunk_cumsum(x):
    ts = x.shape[0]
    tri = jnp.where(_iota((CHUNK, CHUNK), 1) <= _iota((CHUNK, CHUNK), 0), 1.0, 0.0).astype(BF16)
    hi, lo = _split2(x)
    cums = []
    for r0 in range(0, ts, CHUNK):
        cums.append(_mm(tri, hi[r0:r0 + CHUNK]) + _mm(tri, lo[r0:r0 + CHUNK]))
    ends = jnp.concatenate([c[CHUNK - 1:CHUNK] for c in cums], axis=0)
    return cums, ends


def _const_spec(shape):
    nd = len(shape)
    return pl.BlockSpec(shape, lambda *_: (0,) * nd, pipeline_mode=pl.Buffered(1))


def _time_tile(s):
    ts = min(MAX_TIME_TILE, s)
    assert s % ts == 0 and ts % CHUNK == 0
    return ts


def _compiler_params(n_axes):
    return pltpu.CompilerParams(dimension_semantics=("arbitrary",) * n_axes,
                                vmem_limit_bytes=VMEM_LIMIT_BYTES)


def _interleave(stage_fns, n_parts):
    n_stages = len(stage_fns)
    for step in range(n_stages + n_parts - 1):
        live = [stage_fns[step - part](part) for part in range(n_parts) if 0 <= step - part < n_stages]
        while live:
            for gen in list(live):
                if next(gen, StopIteration) is StopIteration:
                    live.remove(gen)


def _rwkv_kernel(un_ref, w_ref, mu_ref, wlr_ref, w0a0_ref, kk_ref, ka_ref, rk_ref, lng_ref, lnb_ref,
                 o_ref, zbuf, ht_ref, kt_ref, rt_ref, kd_ref, bd_ref, ke_ref, be_ref, v_ref, dec_ref,
                 wm_ref, u0_ref, akv_ref, ab_ref, y_ref, gate_ref, bonus_ref):
    ts = un_ref.shape[0]
    hw = 4 * RW_HD
    n_parts = 2
    part_rows = ts // n_parts
    part_chunks = part_rows // CHUNK
    n_in = w_ref.shape[1]

    @pl.when(pl.program_id(1) == 0)
    def _():
        zbuf[0:CARRY_ROWS, :] = jnp.zeros((CARRY_ROWS, RW_SHIFT), F32)
        ht_ref[...] = jnp.zeros(ht_ref.shape, F32)

    col = _iota((CHUNK, hw), 1) % CHUNK
    row = _iota((CHUNK, hw), 0)
    strict = col < row
    incl = col <= row
    eye = jnp.where(col == row, 1.0, 0.0)
    blk = _block_mask((hw, hw), RW_HD, RW_HD)
    bdg = functools.partial(_block_diag, n=4, row_blk=CHUNK, col_blk=RW_HD)
    groups = [slice(hw * gi, hw * (gi + 1)) for gi in range(RW_HEADS // 4)]
    zs = {}

    def rows_of(part):
        return slice(part * part_rows, (part + 1) * part_rows)

    def chunks_of(part):
        return [(part * part_chunks + i, slice(part * part_rows + i * CHUNK, part * part_rows + (i + 1) * CHUNK))
                for i in range(part_chunks)]

    def project(part):
        un = un_ref[rows_of(part), :]
        cols = []
        for c0 in range(0, n_in, 2 * LANES):
            cols.append(_mm(un, w_ref[:, c0:min(c0 + 2 * LANES, n_in)]))
            yield
        zs[part] = jnp.concatenate(cols, axis=-1)

    def prepare(part):
        z = zs.pop(part)
        rows = rows_of(part)
        r0 = part * part_rows
        zsh = z[:, :RW_SHIFT]
        gate_ref[rows, :] = z[:, RW_SHIFT:]
        zbuf[CARRY_ROWS + r0:CARRY_ROWS + r0 + part_rows, :] = zsh
        prev = zbuf[CARRY_ROWS - 1 + r0:CARRY_ROWS - 1 + r0 + part_rows, :]
        if part == n_parts - 1:
            zbuf[0:CARRY_ROWS, :] = zbuf[ts:ts + CARRY_ROWS, :]
        sh = zsh + (prev - zsh) * mu_ref[...]
        yield
        r = sh[:, 0:D_BR]
        k = sh[:, D_BR:2 * D_BR]
        v = sh[:, 2 * D_BR:3 * D_BR]
        lr = sh[:, 3 * D_BR:]
        lr = jnp.where(_iota(lr.shape, 1) < RW_RANK, jnp.tanh(lr), lr)
        pre = _mm(lr, wlr_ref[...]) + w0a0_ref[...]
        yield
        log_w = -_softplus(-pre[:, :D_BR]) - 0.5
        lw = -jnp.exp(log_w)
        yield
        a = _sigmoid(pre[:, D_BR:])
        yield
        kk = k * kk_ref[...]
        kk = kk * lax.rsqrt(jnp.maximum(_seg_sum(kk * kk, RW_HD), 1e-24))
        yield
        k_mod = k * (1.0 + (a - 1.0) * ka_ref[...])
        b = kk * a
        bonus_ref[rows, :] = _seg_sum(r * k_mod * rk_ref[...], RW_HD) * v
        v_ref[rows, :] = v
        yield
        cums, ends = _chunk_cumsum(lw)
        yield
        c = jnp.concatenate(cums, axis=0)
        c_end = jnp.concatenate([jnp.broadcast_to(ends[i:i + 1], (CHUNK, D_BR)) for i in range(part_chunks)], axis=0)
        dec_ref[part * part_chunks:(part + 1) * part_chunks, :] = jnp.exp(ends)
        inv = jnp.exp(-c)
        kd_ref[rows, :] = k_mod * inv
        yield
        bd_ref[rows, :] = b * inv
        yield
        to_end = jnp.exp(c_end - c)
        ke_ref[rows, :] = k_mod * to_end
        yield
        be_ref[rows, :] = -(b * to_end)
        yield
        kt_ref[rows, :] = kk * jnp.exp(c - lw)
        yield
        rt_ref[rows, :] = r * jnp.exp(c)
        yield

    def solve(part):
        inst = [(rows, ls) for _, rows in chunks_of(part) for ls in groups]
        kts = [kt_ref[rows, ls] for rows, ls in inst]
        lhs = [jnp.concatenate([kt, rt_ref[rows, ls]], axis=0) for kt, (rows, ls) in zip(kts, inst)]
        sbs = [_mm_nt(x, bdg(bd_ref[rows, ls])) for x, (rows, ls) in zip(lhs, inst)]
        yield
        sks = [_mm_nt(x, bdg(kd_ref[rows, ls])) for x, (rows, ls) in zip(lhs, inst)]
        for sb, (rows, ls) in zip(sbs, inst):
            ab_ref[rows, ls] = jnp.where(incl, sb[CHUNK:], 0.0)
        yield
        lk_ak = [jnp.concatenate([jnp.where(strict, sk[:CHUNK], 0.0), jnp.where(incl, sk[CHUNK:], 0.0)], axis=0)
                 for sk in sks]
        ms = [jnp.where(strict, -sb[:CHUNK], 0.0) for sb in sbs]
        ps = [eye + m for m in ms]
        ms = [_mm(m, bdg(m)) for m in ms]
        yield
        for _ in range(4):
            sq = [_mm(jnp.concatenate([m, p], axis=0), bdg(m)) for m, p in zip(ms, ps)]
            ps = [p + x[CHUNK:] for p, x in zip(ps, sq)]
            ms = [x[:CHUNK] for x in sq]
            yield
        ps = [p + _mm(p, bdg(m)) for m, p in zip(ms, ps)]
        yield
        xs = [_mm(x, bdg(v_ref[rows, ls])) for x, (rows, ls) in zip(lk_ak, inst)]
        yield
        for p, kt, x, (rows, ls) in zip(ps, kts, xs, inst):
            akv_ref[rows, ls] = x[CHUNK:]
            wm_ref[rows, ls] = _mm(p, bdg(kt))
            u0_ref[rows, ls] = _mm(p, bdg(x[:CHUNK]))
        yield

    def scan(part):
        for ci, rows in chunks_of(part):
            hts = [ht_ref[gi] for gi in range(len(groups))]
            wrs = [_mm_nt(jnp.concatenate([wm_ref[rows, ls], rt_ref[rows, ls]], axis=0), ht)
                   for ht, ls in zip(hts, groups)]
            us = [wr[:CHUNK] + u0_ref[rows, ls] for wr, ls in zip(wrs, groups)]
            upds = [_mm_tn(jnp.concatenate([v_ref[rows, ls], u], axis=0),
                           jnp.concatenate([ke_ref[rows, ls], be_ref[rows, ls]], axis=0)) for u, ls in zip(us, groups)]
            for gi, (ht, upd, ls) in enumerate(zip(hts, upds, groups)):
                ht_ref[gi] = ht * dec_ref[ci:ci + 1, ls] + jnp.where(blk, upd, 0.0)
            for wr, u, ls in zip(wrs, us, groups):
                y_ref[rows, ls] = wr[CHUNK:] + (akv_ref[rows, ls] - _mm(ab_ref[rows, ls], bdg(u)))
            yield

    def finish(part):
        for _, rows in chunks_of(part):
            y = y_ref[rows, :]
            mean = _seg_sum(y, RW_HD) * (1.0 / RW_HD)
            yc = y - mean
            var = _seg_sum(yc * yc, RW_HD) * (1.0 / RW_HD)
            y = yc * lax.rsqrt(var + RW_LN_EPS) * lng_ref[...] + lnb_ref[...]
            o_ref[rows, :] = ((y + bonus_ref[rows, :]) * _silu(gate_ref[rows, :])).astype(o_ref.dtype)
            yield

    _interleave([project, prepare, solve, scan, finish], n_parts)


def _rwkv_branch(un, w, mu, wlr, w0a0, k_k, k_a, r_k, ln_g, ln_b):
    bsz, s, _ = un.shape
    ts = _time_tile(s)
    n_in = w.shape[1]
    vec = lambda n: _const_spec((1, n))
    big = pltpu.VMEM((ts, D_BR), F32)
    return pl.pallas_call(
        _rwkv_kernel,
        grid=(bsz, s // ts),
        in_specs=[pl.BlockSpec((None, ts, D_MODEL), lambda b, t: (b, t, 0)),
                  _const_spec((D_MODEL, n_in)), vec(RW_SHIFT),
                  _const_spec((2 * RW_RANK, 2 * D_BR)), vec(2 * D_BR),
                  vec(D_BR), vec(D_BR), vec(D_BR), vec(D_BR), vec(D_BR)],
        out_specs=pl.BlockSpec((None, ts, D_BR), lambda b, t: (b, t, 0)),
        out_shape=jax.ShapeDtypeStruct((bsz, s, D_BR), BF16),
        scratch_shapes=[pltpu.VMEM((ts + CARRY_ROWS, RW_SHIFT), F32),
                        pltpu.VMEM((RW_HEADS // 4, 4 * RW_HD, 4 * RW_HD), F32),
                        big, big, big, big, big, big, big, pltpu.VMEM((ts // CHUNK, D_BR), F32),
                        big, big, big, big, big, big, big],
        compiler_params=_compiler_params(2),
        name="rwkv_branch",
    )(un, w, mu, wlr, w0a0, k_k, k_a, r_k, ln_g, ln_b)


def _gla_kernel(un_ref, w_ref, fup_ref, fb_ref, ng_ref, o_ref, st_ref, qd_ref, kd_ref, ke_ref, v_ref, dec_ref, y_ref):
    ts = un_ref.shape[0]
    dk = GLA_DK // 4
    dv = D_BR // 4
    n_parts = 2
    part_rows = ts // n_parts
    part_chunks = part_rows // CHUNK
    n_in = w_ref.shape[1]

    @pl.when(pl.program_id(1) == 0)
    def _():
        st_ref[...] = jnp.zeros(st_ref.shape, F32)

    causal = (_iota((CHUNK, 4 * CHUNK), 1) % CHUNK) <= _iota((CHUNK, 4 * CHUNK), 0)
    st_mask = _block_mask((D_BR, GLA_DK), dv, dk)
    zs, gates, atts, upds = {}, {}, {}, {}

    def rows_of(part):
        return slice(part * part_rows, (part + 1) * part_rows)

    def chunks_of(part):
        return [(part * part_chunks + i, slice(part * part_rows + i * CHUNK, part * part_rows + (i + 1) * CHUNK))
                for i in range(part_chunks)]

    def project(part):
        un = un_ref[rows_of(part), :]
        cols = []
        for c0 in range(0, n_in, 2 * LANES):
            cols.append(_mm(un, w_ref[:, c0:min(c0 + 2 * LANES, n_in)]))
            yield
        zs[part] = jnp.concatenate(cols, axis=-1)

    def decays(part):
        z = zs.pop(part)
        rows = rows_of(part)
        gates[part] = z[:, 2 * GLA_DK + D_BR:2 * GLA_DK + 2 * D_BR]
        f = _mm(z[:, 2 * GLA_DK + 2 * D_BR:], fup_ref[...]) + fb_ref[...]
        yield
        log_f = -_softplus(-f) * (1.0 / GLA_TAU)
        yield
        cums, ends = _chunk_cumsum(log_f)
        yield
        bcum = jnp.concatenate(cums, axis=0)
        b_end = jnp.concatenate([jnp.broadcast_to(ends[i:i + 1], (CHUNK, GLA_DK)) for i in range(part_chunks)], axis=0)
        qd_ref[rows, :] = z[:, 0:GLA_DK] * (dk ** -0.5) * jnp.exp(bcum)
        yield
        kd_ref[rows, :] = z[:, GLA_DK:2 * GLA_DK] * jnp.exp(-bcum)
        yield
        ke_ref[rows, :] = z[:, GLA_DK:2 * GLA_DK] * jnp.exp(b_end - bcum)
        yield
        v_ref[rows, :] = z[:, 2 * GLA_DK:2 * GLA_DK + D_BR]
        dec_ref[part * part_chunks:(part + 1) * part_chunks, :] = jnp.exp(ends)
        yield

    def scores(part):
        atts[part], upds[part] = [], []
        for _, rows in chunks_of(part):
            att = _mm_nt(qd_ref[rows, :], _block_diag(kd_ref[rows, :], 4, CHUNK, dk))
            atts[part].append(jnp.where(causal, att, 0.0))
            yield
            upds[part].append(_mm_tn(v_ref[rows, :], ke_ref[rows, :]))
            yield

    def states(part):
        st = st_ref[...]
        for (ci, rows), att, upd in zip(chunks_of(part), atts.pop(part), upds.pop(part)):
            y_ref[rows, :] = _mm(att, _block_diag(v_ref[rows, :], 4, CHUNK, dv)) + _mm_nt(qd_ref[rows, :], st)
            st = st * dec_ref[ci:ci + 1, :] + jnp.where(st_mask, upd, 0.0)
            yield
        st_ref[...] = st

    def finish(part):
        gate = gates.pop(part)
        for i, (_, rows) in enumerate(chunks_of(part)):
            o = y_ref[rows, :]
            o = o * lax.rsqrt(_seg_sum(o * o, dv) * (1.0 / dv) + NORM_EPS)
            o_ref[rows, :] = (o * ng_ref[...] * _silu(gate[i * CHUNK:(i + 1) * CHUNK])).astype(o_ref.dtype)
            yield

    _interleave([project, decays, scores, states, finish], n_parts)


def _gla_branch(un, w, f_up, f_b, norm_g):
    bsz, s, _ = un.shape
    ts = _time_tile(s)
    vec = lambda n: _const_spec((1, n))
    return pl.pallas_call(
        _gla_kernel,
        grid=(bsz, s // ts),
        in_specs=[pl.BlockSpec((None, ts, D_MODEL), lambda b, t: (b, t, 0)),
                  _const_spec(w.shape), _const_spec(f_up.shape), vec(GLA_DK), vec(D_BR)],
        out_specs=pl.BlockSpec((None, ts, D_BR), lambda b, t: (b, t, 0)),
        out_shape=jax.ShapeDtypeStruct((bsz, s, D_BR), BF16),
        scratch_shapes=[pltpu.VMEM((D_BR, GLA_DK), F32),
                        pltpu.VMEM((ts, GLA_DK), F32), pltpu.VMEM((ts, GLA_DK), F32),
                        pltpu.VMEM((ts, GLA_DK), F32), pltpu.VMEM((ts, D_BR), F32),
                        pltpu.VMEM((ts // CHUNK, GLA_DK), F32), pltpu.VMEM((ts, D_BR), F32)],
        compiler_params=_compiler_params(2),
        name="gla_branch",
    )(un, w, f_up, f_b, norm_g)


def _ret_kernel(un_ref, cos_ref, sin_ref, w_ref, gn_ref, dmat_ref, qw_ref, kw_ref, cdec_ref,
                o_ref, st_ref, q_ref, k_ref, ke_ref, v_ref, y_ref):
    ts = un_ref.shape[0]
    dk = RET_DK // 4
    dv = D_BR // 4
    n_parts = 2
    part_rows = ts // n_parts
    part_chunks = part_rows // CHUNK
    n_in = w_ref.shape[1]

    @pl.when(pl.program_id(1) == 0)
    def _():
        st_ref[...] = jnp.zeros(st_ref.shape, F32)

    first2 = (_iota((part_rows, RET_DK), 1) % (2 * ROPE_HALF)) < ROPE_HALF
    st_mask = _block_mask((D_BR, RET_DK), dv, dk)
    zs, gates, atts, upds = {}, {}, {}, {}

    def rows_of(part):
        return slice(part * part_rows, (part + 1) * part_rows)

    def chunks_of(part):
        return [slice(part * part_rows + i * CHUNK, part * part_rows + (i + 1) * CHUNK) for i in range(part_chunks)]

    def project(part):
        un = un_ref[rows_of(part), :]
        cols = []
        for c0 in range(0, n_in, 2 * LANES):
            cols.append(_mm(un, w_ref[:, c0:c0 + 2 * LANES]))
            yield
        zs[part] = jnp.concatenate(cols, axis=-1)

    def rotate(part):
        z = zs.pop(part)
        rows = rows_of(part)
        gates[part] = z[:, 2 * RET_DK + D_BR:]
        v_ref[rows, :] = z[:, 2 * RET_DK:2 * RET_DK + D_BR]
        cos2 = jnp.concatenate([cos_ref[rows, :]] * (RET_DK // LANES), axis=-1)
        sin2 = jnp.concatenate([sin_ref[rows, :]] * (RET_DK // LANES), axis=-1)

        def rope(t):
            partner = jnp.where(first2, pltpu.roll(t, RET_DK - ROPE_HALF, 1), pltpu.roll(t, ROPE_HALF, 1))
            return t * cos2 + partner * sin2

        yield
        q_ref[rows, :] = rope(z[:, 0:RET_DK])
        yield
        kr = rope(z[:, RET_DK:2 * RET_DK]) * (dk ** -0.5)
        k_ref[rows, :] = kr
        yield
        ke_ref[rows, :] = kr * jnp.concatenate([kw_ref[...]] * part_chunks, axis=0)
        yield

    def scores(part):
        atts[part], upds[part] = [], []
        for rows in chunks_of(part):
            atts[part].append(_mm_nt(q_ref[rows, :], _block_diag(k_ref[rows, :], 4, CHUNK, dk)) * dmat_ref[...])
            yield
            upds[part].append(_mm_tn(v_ref[rows, :], ke_ref[rows, :]))
            yield

    def states(part):
        st = st_ref[...]
        for rows, att, upd in zip(chunks_of(part), atts.pop(part), upds.pop(part)):
            y_ref[rows, :] = (_mm(att, _block_diag(v_ref[rows, :], 4, CHUNK, dv))
                              + _mm_nt(q_ref[rows, :], st) * qw_ref[...])
            st = st * cdec_ref[...] + jnp.where(st_mask, upd, 0.0)
            yield
        st_ref[...] = st

    def finish(part):
        gate = gates.pop(part)
        for i, rows in enumerate(chunks_of(part)):
            o = y_ref[rows, :]
            mean = _seg_sum(o, dv) * (1.0 / dv)
            oc = o - mean
            var = _seg_sum(oc * oc, dv) * (1.0 / dv)
            o = oc * lax.rsqrt(var + NORM_EPS)
            o_ref[rows, :] = (o * gn_ref[...] * _silu(gate[i * CHUNK:(i + 1) * CHUNK])).astype(o_ref.dtype)
            yield

    _interleave([project, rotate, scores, states, finish], n_parts)


def _ret_tables():
    h = 4
    dk = RET_DK // h
    dv = D_BR // h
    log_g = jnp.log(1.0 - jnp.exp2(-5.0 - jnp.arange(h, dtype=F32)))
    idx = jnp.arange(CHUNK, dtype=F32)
    diff = idx[:, None] - idx[None, :]
    dmat = jnp.where(diff >= 0, jnp.exp(jnp.maximum(diff, 0.0)[None] * log_g[:, None, None]), 0.0)
    dmat = jnp.transpose(dmat, (1, 0, 2)).reshape(CHUNK, h * CHUNK)
    q_w = jnp.exp((idx + 1.0)[:, None] * log_g)
    k_w = jnp.exp((CHUNK - 1.0 - idx)[:, None] * log_g)
    cdec = jnp.exp(CHUNK * log_g)
    return dmat, jnp.repeat(q_w, dv, axis=1), jnp.repeat(k_w, dk, axis=1), jnp.repeat(cdec, dk)[None, :]


def _rope_kernel(pos_ref, freq_ref, cos_ref, sin_ref):
    ang = pos_ref[...].astype(F32) * freq_ref[...]
    cs = jnp.concatenate([jnp.cos(ang), jnp.sin(ang)], axis=0)
    er = _iota((2 * ROPE_HALF, 2 * LANES), 0)
    ec = _iota((2 * ROPE_HALF, 2 * LANES), 1)
    hit = (ec % ROPE_HALF) == (er % ROPE_HALF)
    sign = jnp.where((ec % (2 * ROPE_HALF)) < ROPE_HALF, -1.0, 1.0)
    spread = jnp.where(hit & (er < ROPE_HALF) & (ec < LANES), 1.0,
                       jnp.where(hit & (er >= ROPE_HALF) & (ec >= LANES), sign, 0.0)).astype(BF16)
    c_hi, c_mid, c_lo = _split3(cs)
    tab = _mm_tn(c_hi, spread) + (_mm_tn(c_mid, spread) + _mm_tn(c_lo, spread))
    cos_ref[...] = tab[:, :LANES]
    sin_ref[...] = tab[:, LANES:]


def _rope_tables(positions):
    bsz, s = positions.shape
    ts = _time_tile(s)
    freq = ROPE_BASE ** (-jnp.arange(ROPE_HALF, dtype=F32) / ROPE_HALF)
    freq = jnp.broadcast_to(freq[:, None], (ROPE_HALF, ts))
    tab = jax.ShapeDtypeStruct((bsz, s, LANES), F32)
    out_spec = pl.BlockSpec((None, ts, LANES), lambda b, t: (b, t, 0))
    return pl.pallas_call(
        _rope_kernel,
        grid=(bsz, s // ts),
        in_specs=[pl.BlockSpec((None, 1, ts), lambda b, t: (b, 0, t)), _const_spec((ROPE_HALF, ts))],
        out_specs=[out_spec, out_spec],
        out_shape=[tab, tab],
        compiler_params=_compiler_params(2),
        name="rope_tables",
    )(positions[:, None, :], freq)


def _ret_branch(un, cos, sin, w, gn_g):
    bsz, s, _ = un.shape
    ts = _time_tile(s)
    dmat, q_w, k_w, cdec = _ret_tables()
    tab = pl.BlockSpec((None, ts, LANES), lambda b, t: (b, t, 0))
    vec = lambda n: _const_spec((1, n))
    return pl.pallas_call(
        _ret_kernel,
        grid=(bsz, s // ts),
        in_specs=[pl.BlockSpec((None, ts, D_MODEL), lambda b, t: (b, t, 0)),
                  tab, tab, _const_spec(w.shape), vec(D_BR),
                  _const_spec(dmat.shape), _const_spec(q_w.shape), _const_spec(k_w.shape), _const_spec(cdec.shape)],
        out_specs=pl.BlockSpec((None, ts, D_BR), lambda b, t: (b, t, 0)),
        out_shape=jax.ShapeDtypeStruct((bsz, s, D_BR), BF16),
        scratch_shapes=[pltpu.VMEM((D_BR, RET_DK), F32),
                        pltpu.VMEM((ts, RET_DK), F32), pltpu.VMEM((ts, RET_DK), F32),
                        pltpu.VMEM((ts, RET_DK), F32), pltpu.VMEM((ts, D_BR), F32),
                        pltpu.VMEM((ts, D_BR), F32)],
        compiler_params=_compiler_params(2),
        name="ret_branch",
    )(un, cos, sin, w, gn_g, dmat, q_w, k_w, cdec)


def _lru_kernel(un_ref, w_ref, cw_ref, cb_ref, wax_ref, bax_ref, lam_ref, o_ref, xbuf, hc_ref):
    ts = un_ref.shape[0]

    @pl.when(pl.program_id(1) == 0)
    def _():
        xbuf[0:CARRY_ROWS, :] = jnp.zeros((CARRY_ROWS, D_BR), F32)
        hc_ref[...] = jnp.zeros(hc_ref.shape, F32)

    z = _mm(un_ref[...], w_ref[...])
    gate = z[:, D_BR:]
    xbuf[CARRY_ROWS:CARRY_ROWS + ts, :] = z[:, :D_BR]
    xc = cb_ref[...] + jnp.zeros((ts, D_BR), F32)
    for j in range(LRU_CONV):
        off = CARRY_ROWS - (LRU_CONV - 1) + j
        xc = xc + xbuf[off:off + ts, :] * cw_ref[j:j + 1, :]
    xbuf[0:CARRY_ROWS, :] = xbuf[ts:ts + CARRY_ROWS, :]

    ri = _sigmoid(_mm(xc, wax_ref[...]) + bax_ref[...])
    log_a = -LRU_C * ri[:, :D_BR] * _softplus(-lam_ref[...])
    a = jnp.exp(log_a)
    u = jnp.sqrt(1.0 - jnp.exp(2.0 * log_a)) * (ri[:, D_BR:] * xc)

    sub = _iota((CARRY_ROWS, D_BR), 0)
    carry = hc_ref[0:1, :]
    blocks = []
    for r0 in range(0, ts, CARRY_ROWS):
        ab = a[r0:r0 + CARRY_ROWS]
        ub = u[r0:r0 + CARRY_ROWS]
        for d in (1, 2, 4):
            keep = sub >= d
            ub = jnp.where(keep, ab * pltpu.roll(ub, d, 0) + ub, ub)
            ab = jnp.where(keep, ab * pltpu.roll(ab, d, 0), ab)
        hb = ub + ab * carry
        carry = hb[CARRY_ROWS - 1:CARRY_ROWS, :]
        blocks.append(hb)
    hh = jnp.concatenate(blocks, axis=0)
    hc_ref[...] = jnp.broadcast_to(hh[ts - 1:ts, :], hc_ref.shape)
    o_ref[...] = (hh * _silu(gate)).astype(o_ref.dtype)


def _lru_branch(un, w, conv_w, conv_b, wax, bax, lam):
    bsz, s, _ = un.shape
    ts = _time_tile(s)
    vec = lambda n: _const_spec((1, n))
    return pl.pallas_call(
        _lru_kernel,
        grid=(bsz, s // ts),
        in_specs=[pl.BlockSpec((None, ts, D_MODEL), lambda b, t: (b, t, 0)),
                  _const_spec(w.shape), _const_spec(conv_w.shape), vec(D_BR),
                  _const_spec(wax.shape), vec(2 * D_BR), vec(D_BR)],
        out_specs=pl.BlockSpec((None, ts, D_BR), lambda b, t: (b, t, 0)),
        out_shape=jax.ShapeDtypeStruct((bsz, s, D_BR), BF16),
        scratch_shapes=[pltpu.VMEM((ts + CARRY_ROWS, D_BR), F32), pltpu.VMEM((CARRY_ROWS, D_BR), F32)],
        compiler_params=_compiler_params(2),
        name="lru_branch",
    )(un, w, conv_w, conv_b, wax, bax, lam)


def _norm_kernel(x_ref, g_ref, o_ref):
    o_ref[...] = _rmsnorm(x_ref[...], g_ref[...]).astype(o_ref.dtype)


def _norm(x, g):
    bsz, s, _ = x.shape
    ts = _time_tile(s)
    ts = ts * (NORM_TILE_SCALE if s % (ts * NORM_TILE_SCALE) == 0 else 1)
    tile = pl.BlockSpec((None, ts, D_MODEL), lambda b, t: (b, t, 0))
    return pl.pallas_call(
        _norm_kernel,
        grid=(bsz, s // ts),
        in_specs=[tile, _const_spec((1, D_MODEL))],
        out_specs=tile,
        out_shape=jax.ShapeDtypeStruct(x.shape, BF16),
        compiler_params=_compiler_params(2),
        name="mix_norm",
    )(x, g)


def _merge_kernel(h_ref, un_ref, y0_ref, y1_ref, y2_ref, y3_ref, gx_ref, wm_ref, wb_ref, wo_ref, o_ref, xn_ref):
    un = un_ref[...]
    merged = None
    for n, y_ref in enumerate((y0_ref, y1_ref, y2_ref, y3_ref)):
        gate = _sigmoid(_mm(un, wm_ref[:, n * D_MODEL:(n + 1) * D_MODEL]))
        term = gate * _mm(y_ref[...], wb_ref[n])
        merged = term if merged is None else merged + term
    out = h_ref[...] + _mm(merged, wo_ref[...])
    o_ref[...] = out
    xn_ref[...] = _rmsnorm(out, gx_ref[...]).astype(xn_ref.dtype)


def _merge(h, un, ys, g_xa, w_merge, w_branch, w_out):
    bsz, s, _ = h.shape
    ts = _time_tile(s)
    tile = lambda n: pl.BlockSpec((None, ts, n), lambda b, t: (b, t, 0))
    return pl.pallas_call(
        _merge_kernel,
        grid=(bsz, s // ts),
        in_specs=[tile(D_MODEL), tile(D_MODEL), tile(D_BR), tile(D_BR), tile(D_BR), tile(D_BR),
                  _const_spec((1, D_MODEL)), _const_spec(w_merge.shape), _const_spec(w_branch.shape),
                  _const_spec(w_out.shape)],
        out_specs=[tile(D_MODEL), tile(D_MODEL)],
        out_shape=[jax.ShapeDtypeStruct(h.shape, F32), jax.ShapeDtypeStruct(h.shape, BF16)],
        compiler_params=_compiler_params(2),
        name="merge_out",
    )(h, un, *ys, g_xa, w_merge, w_branch, w_out)


def _zip_run(*gens):
    live = list(gens)
    while live:
        for gen in list(live):
            if next(gen, StopIteration) is StopIteration:
                live.remove(gen)


def _lru_merge_kernel(h_ref, un_ref, y0_ref, y1_ref, y2_ref, gx_ref, wm_ref, wb_ref, wo_ref,
                      wl_ref, cw_ref, cb_ref, wax_ref, bax_ref, lam_ref, o_ref, xn_ref, xbuf, hc_ref):
    ts = un_ref.shape[0]
    step = 2 * LANES

    @pl.when(pl.program_id(1) == 0)
    def _():
        xbuf[0:CARRY_ROWS, :] = jnp.zeros((CARRY_ROWS, D_BR), F32)
        hc_ref[...] = jnp.zeros(hc_ref.shape, F32)

    un = un_ref[...]
    terms = []
    out = {}

    def merge_matmuls():
        for n, y_ref in enumerate((y0_ref, y1_ref, y2_ref)):
            y = y_ref[...]
            cols = []
            for c0 in range(0, D_MODEL, step):
                gate = _sigmoid(_mm(un, wm_ref[:, n * D_MODEL + c0:n * D_MODEL + c0 + step]))
                yield
                cols.append(gate * _mm(y, wb_ref[n, :, c0:c0 + step]))
                yield
            terms.append(jnp.concatenate(cols, axis=-1))
        cols = []
        for c0 in range(0, D_MODEL, step):
            cols.append(_sigmoid(_mm(un, wm_ref[:, 3 * D_MODEL + c0:3 * D_MODEL + c0 + step])))
            yield
        out["gate3"] = jnp.concatenate(cols, axis=-1)

    def lru():
        z = _mm(un, wl_ref[...])
        yield
        gate = z[:, D_BR:]
        xbuf[CARRY_ROWS:CARRY_ROWS + ts, :] = z[:, :D_BR]
        xc = cb_ref[...] + jnp.zeros((ts, D_BR), F32)
        for j in range(LRU_CONV):
            off = CARRY_ROWS - (LRU_CONV - 1) + j
            xc = xc + xbuf[off:off + ts, :] * cw_ref[j:j + 1, :]
            yield
        xbuf[0:CARRY_ROWS, :] = xbuf[ts:ts + CARRY_ROWS, :]
        ri = _sigmoid(_mm(xc, wax_ref[...]) + bax_ref[...])
        yield
        log_a = -LRU_C * ri[:, :D_BR] * _softplus(-lam_ref[...])
        a = jnp.exp(log_a)
        yield
        u = jnp.sqrt(1.0 - jnp.exp(2.0 * log_a)) * (ri[:, D_BR:] * xc)
        yield
        sub = _iota((CARRY_ROWS, D_BR), 0)
        carry = hc_ref[0:1, :]
        blocks = []
        for r0 in range(0, ts, CARRY_ROWS):
            ab = a[r0:r0 + CARRY_ROWS]
            ub = u[r0:r0 + CARRY_ROWS]
            for d in (1, 2, 4):
                keep = sub >= d
                ub = jnp.where(keep, ab * pltpu.roll(ub, d, 0) + ub, ub)
                ab = jnp.where(keep, ab * pltpu.roll(ab, d, 0), ab)
            hb = ub + ab * carry
            carry = hb[CARRY_ROWS - 1:CARRY_ROWS, :]
            blocks.append(hb)
            if (r0 // CARRY_ROWS) % 4 == 3:
                yield
        hh = jnp.concatenate(blocks, axis=0)
        hc_ref[...] = jnp.broadcast_to(hh[ts - 1:ts, :], hc_ref.shape)
        out["y_lru"] = hh * _silu(gate)

    _zip_run(merge_matmuls(), lru())
    merged = terms[0] + terms[1] + terms[2] + out["gate3"] * _mm(out["y_lru"], wb_ref[3])
    res = h_ref[...] + _mm(merged, wo_ref[...])
    o_ref[...] = res
    xn_ref[...] = _rmsnorm(res, gx_ref[...]).astype(xn_ref.dtype)


def _lru_merge(h, un, ys, g_xa, w_merge, w_branch, w_out, w_lru, conv_w, conv_b, wax, bax, lam):
    bsz, s, _ = h.shape
    ts = _time_tile(s)
    vec = lambda n: _const_spec((1, n))
    tile = lambda n: pl.BlockSpec((None, ts, n), lambda b, t: (b, t, 0))
    return pl.pallas_call(
        _lru_merge_kernel,
        grid=(bsz, s // ts),
        in_specs=[tile(D_MODEL), tile(D_MODEL), tile(D_BR), tile(D_BR), tile(D_BR),
                  vec(D_MODEL), _const_spec(w_merge.shape), _const_spec(w_branch.shape), _const_spec(w_out.shape),
                  _const_spec(w_lru.shape), _const_spec(conv_w.shape), vec(D_BR), _const_spec(wax.shape),
                  vec(2 * D_BR), vec(D_BR)],
        out_specs=[tile(D_MODEL), tile(D_MODEL)],
        out_shape=[jax.ShapeDtypeStruct(h.shape, F32), jax.ShapeDtypeStruct(h.shape, BF16)],
        scratch_shapes=[pltpu.VMEM((ts + CARRY_ROWS, D_BR), F32), pltpu.VMEM((CARRY_ROWS, D_BR), F32)],
        compiler_params=_compiler_params(2),
        name="lru_merge_out",
    )(h, un, *ys, g_xa, w_merge, w_branch, w_out, w_lru, conv_w, conv_b, wax, bax, lam)


def _xattn_kernel(h_ref, xn_ref, mem_ref, gm_ref, wq_ref, wkv_ref, wo_ref, gn_ref, *rest, final_norm):
    if final_norm:
        o_ref, k_ref, v_ref = rest
    else:
        o_ref, un_ref, k_ref, v_ref = rest

    @pl.when(pl.program_id(1) == 0)
    def _():
        mn = _rmsnorm(mem_ref[...], gm_ref[...])
        kv = _mm(mn, wkv_ref[...])
        k_ref[...] = kv[:, :D_MODEL].astype(BF16)
        v_ref[...] = kv[:, D_MODEL:].astype(BF16)

    ts = h_ref.shape[0]
    n_parts = 2
    part_rows = ts // n_parts
    step = 2 * LANES
    qs, attn, res = {}, {}, {}

    def rows_of(part):
        return slice(part * part_rows, (part + 1) * part_rows)

    def queries(part):
        xn = xn_ref[rows_of(part), :]
        cols = []
        for c0 in range(0, D_MODEL, step):
            cols.append(_mm(xn, wq_ref[:, c0:c0 + step]))
            yield
        qs[part] = cols

    def heads(part):
        outs = []
        for hd, q in enumerate(qs.pop(part)):
            ls = slice(hd * XA_HD, (hd + 1) * XA_HD)
            sc = _mm_nt(q, k_ref[:, ls]) * (XA_HD ** -0.5)
            yield
            sc = sc - jnp.max(sc, axis=-1, keepdims=True)
            e = jnp.exp(sc)
            yield
            pr = e / jnp.sum(e, axis=-1, keepdims=True)
            yield
            outs.append(_mm(pr, v_ref[:, ls]))
            yield
        attn[part] = jnp.concatenate(outs, axis=-1)

    def project(part):
        o = attn.pop(part)
        cols = []
        for c0 in range(0, D_MODEL, step):
            cols.append(_mm(o, wo_ref[:, c0:c0 + step]))
            yield
        res[part] = h_ref[rows_of(part), :] + jnp.concatenate(cols, axis=-1)

    def finish(part):
        rows = rows_of(part)
        out = res.pop(part)
        normed = _rmsnorm(out, gn_ref[...])
        if final_norm:
            o_ref[rows, :] = normed
        else:
            o_ref[rows, :] = out
            un_ref[rows, :] = normed.astype(un_ref.dtype)
        yield

    assert step == XA_HD
    _interleave([queries, heads, project, finish], n_parts)


def _xattn(h, xn, mem, g_mem, wq, wkv, wo, g_next, final_norm):
    bsz, s, _ = h.shape
    n_mem = mem.shape[1]
    ts = _time_tile(s)
    vec = lambda n: _const_spec((1, n))
    tile = pl.BlockSpec((None, ts, D_MODEL), lambda b, t: (b, t, 0))
    out_f32 = jax.ShapeDtypeStruct(h.shape, F32)
    return pl.pallas_call(
        functools.partial(_xattn_kernel, final_norm=final_norm),
        grid=(bsz, s // ts),
        in_specs=[tile, tile, pl.BlockSpec((None, n_mem, D_MODEL), lambda b, t: (b, 0, 0)),
                  vec(D_MODEL), _const_spec(wq.shape), _const_spec(wkv.shape), _const_spec(wo.shape),
                  vec(D_MODEL)],
        out_specs=tile if final_norm else [tile, tile],
        out_shape=out_f32 if final_norm else [out_f32, jax.ShapeDtypeStruct(h.shape, BF16)],
        scratch_shapes=[pltpu.VMEM((n_mem, D_MODEL), BF16), pltpu.VMEM((n_mem, D_MODEL), BF16)],
        compiler_params=_compiler_params(2),
        name="xattn_final" if final_norm else "xattn",
    )(h, xn, mem, g_mem, wq, wkv, wo, g_next)


def _row(v):
    return v.reshape(1, -1).astype(F32)


def _dense_block_diag(w):
    n, c, d = w.shape
    eye = jnp.eye(n, dtype=w.dtype)
    return (eye[:, None, :, None] * w[:, :, None, :]).reshape(n * c, n * d)


def kernel(x, mem, positions, mix_norm_g, w_in, rw_mu, rw_w0, rw_w2, rw_a0, rw_a2, rw_k_k, rw_k_a, rw_r_k, rw_ln_g, rw_ln_b, gla_f_up, gla_f_b, gla_norm_g, ret_gn_g, lru_conv_w, lru_conv_b, lru_wa, lru_ba, lru_wx, lru_bx, lru_lambda, w_branch, w_out, xa_norm_g, xa_mem_norm_g, xa_wq, xa_wkv, xa_wo, final_norm_g):
    depth = w_in.shape[0]
    cos, sin = _rope_tables(positions)
    n_rw = RW_SHIFT + D_BR
    n_gla = 2 * GLA_DK + 2 * D_BR + GLA_RANK
    n_ret = 2 * RET_DK + 2 * D_BR
    o_gla = n_rw
    o_ret = o_gla + n_gla
    o_lru = o_ret + n_ret
    o_mrg = o_lru + 2 * D_BR
    h = x
    un = _norm(x, _row(mix_norm_g[0]))
    for l in range(depth):
        wl = w_in[l].astype(BF16)
        last = l == depth - 1

        z64 = jnp.zeros((RW_RANK, D_BR), F32)
        wlr = jnp.concatenate([jnp.concatenate([rw_w2[l], z64], axis=1),
                               jnp.concatenate([z64, rw_a2[l]], axis=1)], axis=0).astype(BF16)
        y_rw = _rwkv_branch(un, wl[:, :n_rw], _row(rw_mu[l]), wlr,
                            _row(jnp.concatenate([rw_w0[l], rw_a0[l]])), _row(rw_k_k[l]), _row(rw_k_a[l]),
                            _row(rw_r_k[l]), _row(rw_ln_g[l]), _row(rw_ln_b[l]))

        o_f = o_gla + 2 * GLA_DK + D_BR
        w_gla = jnp.concatenate([wl[:, o_gla:o_f], wl[:, o_f + GLA_RANK:o_ret], wl[:, o_f:o_f + GLA_RANK],
                                 jnp.zeros((D_MODEL, LANES - GLA_RANK), BF16)], axis=1)
        f_up = jnp.concatenate([gla_f_up[l], jnp.zeros((LANES - GLA_RANK, GLA_DK), F32)], axis=0).astype(BF16)
        y_gla = _gla_branch(un, w_gla, f_up, _row(gla_f_b[l]), _row(gla_norm_g[l]))

        y_ret = _ret_branch(un, cos, sin, wl[:, o_ret:o_lru], _row(ret_gn_g[l]))

        wax = jnp.concatenate([_dense_block_diag(lru_wa[l]), _dense_block_diag(lru_wx[l])], axis=1).astype(BF16)
        h, xn = _lru_merge(h, un, (y_rw, y_gla, y_ret), _row(xa_norm_g[l]), wl[:, o_mrg:],
                           w_branch[l].astype(BF16), w_out[l].astype(BF16), wl[:, o_lru:o_mrg],
                           lru_conv_w[l].astype(F32), _row(lru_conv_b[l]), wax,
                           _row(jnp.concatenate([lru_ba[l], lru_bx[l]])), _row(lru_lambda[l]))
        g_next = _row(final_norm_g if last else mix_norm_g[l + 1])
        res = _xattn(h, xn, mem, _row(xa_mem_norm_g[l]), xa_wq[l].astype(BF16), xa_wkv[l].astype(BF16),
                     xa_wo[l].astype(BF16), g_next, last)
        if last:
            return res
        h, un = res
```

```python
import functools

import jax
import jax.numpy as jnp
from jax import lax
from jax.experimental import pallas as pl
from jax.experimental.pallas import tpu as pltpu

F32 = jnp.float32
BF16 = jnp.bfloat16

D_MODEL = 1024
D_BR = 512
RW_HEADS = 8
RW_HD = D_BR // RW_HEADS
RW_RANK = 64
RW_LN_EPS = 64e-5
RW_SHIFT = 3 * D_BR + 2 * RW_RANK
GLA_DK = D_BR // 2
GLA_RANK = 16
GLA_TAU = 16.0
RET_DK = D_BR // 2
ROPE_BASE = 10000.0
ROPE_HALF = 32
LRU_CONV = 4
LRU_C = 8.0
XA_HEADS = 4
XA_HD = D_MODEL // XA_HEADS
NORM_EPS = 1e-6
CHUNK = 64
LANES = 128
CARRY_ROWS = 8
MAX_TIME_TILE = 512
PROLOGUE_TILE_SCALE = 4
LIGHT_TILE_SCALE = 2
VMEM_LIMIT_BYTES = 56 * 1024 * 1024


def _mm(a, b):
    return jnp.dot(a.astype(BF16), b.astype(BF16), preferred_element_type=F32)


def _mm_nt(a, b):
    return lax.dot_general(a.astype(BF16), b.astype(BF16), (((1,), (1,)), ((), ())),
                           preferred_element_type=F32)


def _mm_tn(a, b):
    return lax.dot_general(a.astype(BF16), b.astype(BF16), (((0,), (0,)), ((), ())),
                           preferred_element_type=F32)


def _split2(x):
    hi = x.astype(BF16)
    lo = (x - hi.astype(F32)).astype(BF16)
    return hi, lo


def _split3(x):
    hi = x.astype(BF16)
    r = x - hi.astype(F32)
    mid = r.astype(BF16)
    lo = (r - mid.astype(F32)).astype(BF16)
    return hi, mid, lo


def _rmsnorm(x, g):
    ms = jnp.mean(x * x, axis=-1, keepdims=True)
    return x * lax.rsqrt(ms + NORM_EPS) * g


def _sigmoid(x):
    return 1.0 / (1.0 + jnp.exp(-x))


def _softplus(x):
    return jnp.maximum(x, 0.0) + jnp.log(1.0 + jnp.exp(-jnp.abs(x)))


def _silu(x):
    return x * _sigmoid(x)


def _iota(shape, dim):
    return lax.broadcasted_iota(jnp.int32, shape, dim)


def _seg_sum(x, width):
    pieces = []
    for j in range(x.shape[-1] // LANES):
        xs = x[:, j * LANES:(j + 1) * LANES]
        if width == LANES:
            s = jnp.sum(xs, axis=-1, keepdims=True)
            pieces.append(jnp.broadcast_to(s, xs.shape))
        else:
            low = _iota(xs.shape, 1) < width
            s_lo = jnp.sum(jnp.where(low, xs, 0.0), axis=-1, keepdims=True)
            s_hi = jnp.sum(jnp.where(low, 0.0, xs), axis=-1, keepdims=True)
            pieces.append(jnp.where(low, s_lo, s_hi))
    return jnp.concatenate(pieces, axis=-1)


def _block_diag(x, n, row_blk, col_blk):
    t = jnp.concatenate([x] * n, axis=0)
    keep = (_iota(t.shape, 0) // row_blk) == (_iota(t.shape, 1) // col_blk)
    return jnp.where(keep, t, jnp.zeros_like(t))


def _block_mask(shape, row_blk, col_blk):
    return (_iota(shape, 0) // row_blk) == (_iota(shape, 1) // col_blk)


def _chunk_cumsum(x):
    ts = x.shape[0]
    tri = jnp.where(_iota((CHUNK, CHUNK), 1) <= _iota((CHUNK, CHUNK), 0), 1.0, 0.0).astype(BF16)
    hi, lo = _split2(x)
    cums = []
    for r0 in range(0, ts, CHUNK):
        cums.append(_mm(tri, hi[r0:r0 + CHUNK]) + _mm(tri, lo[r0:r0 + CHUNK]))
    ends = jnp.concatenate([c[CHUNK - 1:CHUNK] for c in cums], axis=0)
    return cums, ends


def _const_spec(shape):
    nd = len(shape)
    return pl.BlockSpec(shape, lambda *_: (0,) * nd, pipeline_mode=pl.Buffered(1))


def _time_tile(s, scale=1):
    ts = min(MAX_TIME_TILE * scale, s)
    assert s % ts == 0 and ts % CHUNK == 0
    return ts


def _compiler_params(n_axes):
    return pltpu.CompilerParams(dimension_semantics=("arbitrary",) * n_axes,
                                vmem_limit_bytes=VMEM_LIMIT_BYTES)


def _interleave(stage_fns, n_parts):
    n_stages = len(stage_fns)
    for step in range(n_stages + n_parts - 1):
        live = [stage_fns[step - part](part) for part in range(n_parts) if 0 <= step - part < n_stages]
        while live:
            for gen in list(live):
                if next(gen, StopIteration) is StopIteration:
                    live.remove(gen)


def _rwkv_kernel(un_ref, w_ref, mu_ref, wlr_ref, w0a0_ref, kk_ref, ka_ref, rk_ref, lng_ref, lnb_ref,
                 o_ref, zbuf, ht_ref, kt_ref, rt_ref, kd_ref, bd_ref, ke_ref, be_ref, v_ref, dec_ref,
                 wm_ref, u0_ref, akv_ref, ab_ref, y_ref, gate_ref, bonus_ref):
    ts = un_ref.shape[0]
    hw = 4 * RW_HD
    n_parts = 2
    part_rows = ts // n_parts
    part_chunks = part_rows // CHUNK
    n_in = w_ref.shape[1]

    @pl.when(pl.program_id(1) == 0)
    def _():
        zbuf[0:CARRY_ROWS, :] = jnp.zeros((CARRY_ROWS, RW_SHIFT), F32)
        ht_ref[...] = jnp.zeros(ht_ref.shape, F32)

    col = _iota((CHUNK, hw), 1) % CHUNK
    row = _iota((CHUNK, hw), 0)
    strict = col < row
    incl = col <= row
    eye = jnp.where(col == row, 1.0, 0.0)
    blk = _block_mask((hw, hw), RW_HD, RW_HD)
    bdg = functools.partial(_block_diag, n=4, row_blk=CHUNK, col_blk=RW_HD)
    groups = [slice(hw * gi, hw * (gi + 1)) for gi in range(RW_HEADS // 4)]
    zs = {}

    def rows_of(part):
        return slice(part * part_rows, (part + 1) * part_rows)

    def chunks_of(part):
        return [(part * part_chunks + i, slice(part * part_rows + i * CHUNK, part * part_rows + (i + 1) * CHUNK))
                for i in range(part_chunks)]

    def project(part):
        un = un_ref[rows_of(part), :]
        cols = []
        for c0 in range(0, n_in, 2 * LANES):
            cols.append(_mm(un, w_ref[:, c0:min(c0 + 2 * LANES, n_in)]))
            yield
        zs[part] = jnp.concatenate(cols, axis=-1)

    def prepare(part):
        z = zs.pop(part)
        rows = rows_of(part)
        r0 = part * part_rows
        zsh = z[:, :RW_SHIFT]
        gate_ref[rows, :] = z[:, RW_SHIFT:]
        zbuf[CARRY_ROWS + r0:CARRY_ROWS + r0 + part_rows, :] = zsh
        prev = zbuf[CARRY_ROWS - 1 + r0:CARRY_ROWS - 1 + r0 + part_rows, :]
        if part == n_parts - 1:
            zbuf[0:CARRY_ROWS, :] = zbuf[ts:ts + CARRY_ROWS, :]
        sh = zsh + (prev - zsh) * mu_ref[...]
        yield
        r = sh[:, 0:D_BR]
        k = sh[:, D_BR:2 * D_BR]
        v = sh[:, 2 * D_BR:3 * D_BR]
        lr = sh[:, 3 * D_BR:]
        lr = jnp.where(_iota(lr.shape, 1) < RW_RANK, jnp.tanh(lr), lr)
        pre = _mm(lr, wlr_ref[...]) + w0a0_ref[...]
        yield
        log_w = -_softplus(-pre[:, :D_BR]) - 0.5
        lw = -jnp.exp(log_w)
        yield
        a = _sigmoid(pre[:, D_BR:])
        yield
        kk = k * kk_ref[...]
        kk = kk * lax.rsqrt(jnp.maximum(_seg_sum(kk * kk, RW_HD), 1e-24))
        yield
        k_mod = k * (1.0 + (a - 1.0) * ka_ref[...])
        b = kk * a
        bonus_ref[rows, :] = _seg_sum(r * k_mod * rk_ref[...], RW_HD) * v
        v_ref[rows, :] = v
        yield
        cums, ends = _chunk_cumsum(lw)
        yield
        c = jnp.concatenate(cums, axis=0)
        c_end = jnp.concatenate([jnp.broadcast_to(ends[i:i + 1], (CHUNK, D_BR)) for i in range(part_chunks)], axis=0)
        dec_ref[part * part_chunks:(part + 1) * part_chunks, :] = jnp.exp(ends)
        inv = jnp.exp(-c)
        kd_ref[rows, :] = k_mod * inv
        yield
        bd_ref[rows, :] = b * inv
        yield
        to_end = jnp.exp(c_end - c)
        ke_ref[rows, :] = k_mod * to_end
        yield
        be_ref[rows, :] = -(b * to_end)
        yield
        kt_ref[rows, :] = kk * jnp.exp(c - lw)
        yield
        rt_ref[rows, :] = r * jnp.exp(c)
        yield

    def solve(part):
        inst = [(rows, ls) for _, rows in chunks_of(part) for ls in groups]
        kts = [kt_ref[rows, ls] for rows, ls in inst]
        lhs = [jnp.concatenate([kt, rt_ref[rows, ls]], axis=0) for kt, (rows, ls) in zip(kts, inst)]
        sbs = [_mm_nt(x, bdg(bd_ref[rows, ls])) for x, (rows, ls) in zip(lhs, inst)]
        yield
        sks = [_mm_nt(x, bdg(kd_ref[rows, ls])) for x, (rows, ls) in zip(lhs, inst)]
        for sb, (rows, ls) in zip(sbs, inst):
            ab_ref[rows, ls] = jnp.where(incl, sb[CHUNK:], 0.0)
        yield
        lk_ak = [jnp.concatenate([jnp.where(strict, sk[:CHUNK], 0.0), jnp.where(incl, sk[CHUNK:], 0.0)], axis=0)
                 for sk in sks]
        ms = [jnp.where(strict, -sb[:CHUNK], 0.0) for sb in sbs]
        ps = [eye + m for m in ms]
        ms = [_mm(m, bdg(m)) for m in ms]
        yield
        for _ in range(4):
            sq = [_mm(jnp.concatenate([m, p], axis=0), bdg(m)) for m, p in zip(ms, ps)]
            ps = [p + x[CHUNK:] for p, x in zip(ps, sq)]
            ms = [x[:CHUNK] for x in sq]
            yield
        ps = [p + _mm(p, bdg(m)) for m, p in zip(ms, ps)]
        yield
        xs = [_mm(x, bdg(v_ref[rows, ls])) for x, (rows, ls) in zip(lk_ak, inst)]
        yield
        for p, kt, x, (rows, ls) in zip(ps, kts, xs, inst):
            akv_ref[rows, ls] = x[CHUNK:]
            wm_ref[rows, ls] = _mm(p, bdg(kt))
            u0_ref[rows, ls] = _mm(p, bdg(x[:CHUNK]))
        yield

    def scan(part):
        for ci, rows in chunks_of(part):
            hts = [ht_ref[gi] for gi in range(len(groups))]
            wrs = [_mm_nt(jnp.concatenate([wm_ref[rows, ls], rt_ref[rows, ls]], axis=0), ht)
                   for ht, ls in zip(hts, groups)]
            us = [wr[:CHUNK] + u0_ref[rows, ls] for wr, ls in zip(wrs, groups)]
            upds = [_mm_tn(jnp.concatenate([v_ref[rows, ls], u], axis=0),
                           jnp.concatenate([ke_ref[rows, ls], be_ref[rows, ls]], axis=0)) for u, ls in zip(us, groups)]
            for gi, (ht, upd, ls) in enumerate(zip(hts, upds, groups)):
                ht_ref[gi] = ht * dec_ref[ci:ci + 1, ls] + jnp.where(blk, upd, 0.0)
            for wr, u, ls in zip(wrs, us, groups):
                y_ref[rows, ls] = wr[CHUNK:] + (akv_ref[rows, ls] - _mm(ab_ref[rows, ls], bdg(u)))
            yield

    def finish(part):
        for _, rows in chunks_of(part):
            y = y_ref[rows, :]
            mean = _seg_sum(y, RW_HD) * (1.0 / RW_HD)
            yc = y - mean
            var = _seg_sum(yc * yc, RW_HD) * (1.0 / RW_HD)
            y = yc * lax.rsqrt(var + RW_LN_EPS) * lng_ref[...] + lnb_ref[...]
            o_ref[rows, :] = ((y + bonus_ref[rows, :]) * _silu(gate_ref[rows, :])).astype(o_ref.dtype)
            yield

    _interleave([project, prepare, solve, scan, finish], n_parts)


def _rwkv_branch(un, w, mu, wlr, w0a0, k_k, k_a, r_k, ln_g, ln_b):
    bsz, s, _ = un.shape
    ts = _time_tile(s)
    n_in = w.shape[1]
    vec = lambda n: _const_spec((1, n))
    big = pltpu.VMEM((ts, D_BR), F32)
    return pl.pallas_call(
        _rwkv_kernel,
        grid=(bsz, s // ts),
        in_specs=[pl.BlockSpec((None, ts, D_MODEL), lambda b, t: (b, t, 0)),
                  _const_spec((D_MODEL, n_in)), vec(RW_SHIFT),
                  _const_spec((2 * RW_RANK, 2 * D_BR)), vec(2 * D_BR),
                  vec(D_BR), vec(D_BR), vec(D_BR), vec(D_BR), vec(D_BR)],
        out_specs=pl.BlockSpec((None, ts, D_BR), lambda b, t: (b, t, 0)),
        out_shape=jax.ShapeDtypeStruct((bsz, s, D_BR), BF16),
        scratch_shapes=[pltpu.VMEM((ts + CARRY_ROWS, RW_SHIFT), F32),
                        pltpu.VMEM((RW_HEADS // 4, 4 * RW_HD, 4 * RW_HD), F32),
                        big, big, big, big, big, big, big, pltpu.VMEM((ts // CHUNK, D_BR), F32),
                        big, big, big, big, big, big, big],
        compiler_params=_compiler_params(2),
        name="rwkv_branch",
    )(un, w, mu, wlr, w0a0, k_k, k_a, r_k, ln_g, ln_b)


def _gla_kernel(un_ref, w_ref, fup_ref, fb_ref, ng_ref, o_ref, st_ref, qd_ref, kd_ref, ke_ref, v_ref, dec_ref, y_ref):
    ts = un_ref.shape[0]
    dk = GLA_DK // 4
    dv = D_BR // 4
    n_parts = 2
    part_rows = ts // n_parts
    part_chunks = part_rows // CHUNK
    n_in = w_ref.shape[1]

    @pl.when(pl.program_id(1) == 0)
    def _():
        st_ref[...] = jnp.zeros(st_ref.shape, F32)

    causal = (_iota((CHUNK, 4 * CHUNK), 1) % CHUNK) <= _iota((CHUNK, 4 * CHUNK), 0)
    st_mask = _block_mask((D_BR, GLA_DK), dv, dk)
    zs, gates, atts, upds = {}, {}, {}, {}

    def rows_of(part):
        return slice(part * part_rows, (part + 1) * part_rows)

    def chunks_of(part):
        return [(part * part_chunks + i, slice(part * part_rows + i * CHUNK, part * part_rows + (i + 1) * CHUNK))
                for i in range(part_chunks)]

    def project(part):
        un = un_ref[rows_of(part), :]
        cols = []
        for c0 in range(0, n_in, 2 * LANES):
            cols.append(_mm(un, w_ref[:, c0:min(c0 + 2 * LANES, n_in)]))
            yield
        zs[part] = jnp.concatenate(cols, axis=-1)

    def decays(part):
        z = zs.pop(part)
        rows = rows_of(part)
        gates[part] = z[:, 2 * GLA_DK + D_BR:2 * GLA_DK + 2 * D_BR]
        f = _mm(z[:, 2 * GLA_DK + 2 * D_BR:], fup_ref[...]) + fb_ref[...]
        yield
        log_f = -_softplus(-f) * (1.0 / GLA_TAU)
        yield
        cums, ends = _chunk_cumsum(log_f)
        yield
        bcum = jnp.concatenate(cums, axis=0)
        b_end = jnp.concatenate([jnp.broadcast_to(ends[i:i + 1], (CHUNK, GLA_DK)) for i in range(part_chunks)], axis=0)
        qd_ref[rows, :] = z[:, 0:GLA_DK] * (dk ** -0.5) * jnp.exp(bcum)
        yield
        kd_ref[rows, :] = z[:, GLA_DK:2 * GLA_DK] * jnp.exp(-bcum)
        yield
        ke_ref[rows, :] = z[:, GLA_DK:2 * GLA_DK] * jnp.exp(b_end - bcum)
        yield
        v_ref[rows, :] = z[:, 2 * GLA_DK:2 * GLA_DK + D_BR]
        dec_ref[part * part_chunks:(part + 1) * part_chunks, :] = jnp.exp(ends)
        yield

    def scores(part):
        atts[part], upds[part] = [], []
        for _, rows in chunks_of(part):
            att = _mm_nt(qd_ref[rows, :], _block_diag(kd_ref[rows, :], 4, CHUNK, dk))
            atts[part].append(jnp.where(causal, att, 0.0))
            yield
            upds[part].append(_mm_tn(v_ref[rows, :], ke_ref[rows, :]))
            yield

    def states(part):
        st = st_ref[...]
        for (ci, rows), att, upd in zip(chunks_of(part), atts.pop(part), upds.pop(part)):
            y_ref[rows, :] = _mm(att, _block_diag(v_ref[rows, :], 4, CHUNK, dv)) + _mm_nt(qd_ref[rows, :], st)
            st = st * dec_ref[ci:ci + 1, :] + jnp.where(st_mask, upd, 0.0)
            yield
        st_ref[...] = st

    def finish(part):
        gate = gates.pop(part)
        for i, (_, rows) in enumerate(chunks_of(part)):
            o = y_ref[rows, :]
            o = o * lax.rsqrt(_seg_sum(o * o, dv) * (1.0 / dv) + NORM_EPS)
            o_ref[rows, :] = (o * ng_ref[...] * _silu(gate[i * CHUNK:(i + 1) * CHUNK])).astype(o_ref.dtype)
            yield

    _interleave([project, decays, scores, states, finish], n_parts)


def _gla_branch(un, w, f_up, f_b, norm_g):
    bsz, s, _ = un.shape
    ts = _time_tile(s, LIGHT_TILE_SCALE)
    vec = lambda n: _const_spec((1, n))
    return pl.pallas_call(
        _gla_kernel,
        grid=(bsz, s // ts),
        in_specs=[pl.BlockSpec((None, ts, D_MODEL), lambda b, t: (b, t, 0)),
                  _const_spec(w.shape), _const_spec(f_up.shape), vec(GLA_DK), vec(D_BR)],
        out_specs=pl.BlockSpec((None, ts, D_BR), lambda b, t: (b, t, 0)),
        out_shape=jax.ShapeDtypeStruct((bsz, s, D_BR), BF16),
        scratch_shapes=[pltpu.VMEM((D_BR, GLA_DK), F32),
                        pltpu.VMEM((ts, GLA_DK), F32), pltpu.VMEM((ts, GLA_DK), F32),
                        pltpu.VMEM((ts, GLA_DK), F32), pltpu.VMEM((ts, D_BR), F32),
                        pltpu.VMEM((ts // CHUNK, GLA_DK), F32), pltpu.VMEM((ts, D_BR), F32)],
        compiler_params=_compiler_params(2),
        name="gla_branch",
    )(un, w, f_up, f_b, norm_g)


def _ret_kernel(un_ref, cos_ref, sin_ref, w_ref, gn_ref, dmat_ref, qw_ref, kw_ref, cdec_ref,
                o_ref, st_ref, q_ref, k_ref, ke_ref, v_ref, y_ref):
    ts = un_ref.shape[0]
    dk = RET_DK // 4
    dv = D_BR // 4
    n_parts = 2
    part_rows = ts // n_parts
    part_chunks = part_rows // CHUNK
    n_in = w_ref.shape[1]

    @pl.when(pl.program_id(1) == 0)
    def _():
        st_ref[...] = jnp.zeros(st_ref.shape, F32)

    first2 = (_iota((part_rows, RET_DK), 1) % (2 * ROPE_HALF)) < ROPE_HALF
    st_mask = _block_mask((D_BR, RET_DK), dv, dk)
    zs, gates, atts, upds = {}, {}, {}, {}

    def rows_of(part):
        return slice(part * part_rows, (part + 1) * part_rows)

    def chunks_of(part):
        return [slice(part * part_rows + i * CHUNK, part * part_rows + (i + 1) * CHUNK) for i in range(part_chunks)]

    def project(part):
        un = un_ref[rows_of(part), :]
        cols = []
        for c0 in range(0, n_in, 2 * LANES):
            cols.append(_mm(un, w_ref[:, c0:c0 + 2 * LANES]))
            yield
        zs[part] = jnp.concatenate(cols, axis=-1)

    def rotate(part):
        z = zs.pop(part)
        rows = rows_of(part)
        gates[part] = z[:, 2 * RET_DK + D_BR:]
        v_ref[rows, :] = z[:, 2 * RET_DK:2 * RET_DK + D_BR]
        cos2 = jnp.concatenate([cos_ref[rows, :]] * (RET_DK // LANES), axis=-1)
        sin2 = jnp.concatenate([sin_ref[rows, :]] * (RET_DK // LANES), axis=-1)

        def rope(t):
            partner = jnp.where(first2, pltpu.roll(t, RET_DK - ROPE_HALF, 1), pltpu.roll(t, ROPE_HALF, 1))
            return t * cos2 + partner * sin2

        yield
        q_ref[rows, :] = rope(z[:, 0:RET_DK])
        yield
        kr = rope(z[:, RET_DK:2 * RET_DK]) * (dk ** -0.5)
        k_ref[rows, :] = kr
        yield
        ke_ref[rows, :] = kr * jnp.concatenate([kw_ref[...]] * part_chunks, axis=0)
        yield

    def scores(part):
        atts[part], upds[part] = [], []
        for rows in chunks_of(part):
            atts[part].append(_mm_nt(q_ref[rows, :], _block_diag(k_ref[rows, :], 4, CHUNK, dk)) * dmat_ref[...])
            yield
            upds[part].append(_mm_tn(v_ref[rows, :], ke_ref[rows, :]))
            yield

    def states(part):
        st = st_ref[...]
        for rows, att, upd in zip(chunks_of(part), atts.pop(part), upds.pop(part)):
            y_ref[rows, :] = (_mm(att, _block_diag(v_ref[rows, :], 4, CHUNK, dv))
                              + _mm_nt(q_ref[rows, :], st) * qw_ref[...])
            st = st * cdec_ref[...] + jnp.where(st_mask, upd, 0.0)
            yield
        st_ref[...] = st

    def finish(part):
        gate = gates.pop(part)
        for i, rows in enumerate(chunks_of(part)):
            o = y_ref[rows, :]
            mean = _seg_sum(o, dv) * (1.0 / dv)
            oc = o - mean
            var = _seg_sum(oc * oc, dv) * (1.0 / dv)
            o = oc * lax.rsqrt(var + NORM_EPS)
            o_ref[rows, :] = (o * gn_ref[...] * _silu(gate[i * CHUNK:(i + 1) * CHUNK])).astype(o_ref.dtype)
            yield

    _interleave([project, rotate, scores, states, finish], n_parts)


def _ret_tables():
    h = 4
    dk = RET_DK // h
    dv = D_BR // h
    log_g = jnp.log(1.0 - jnp.exp2(-5.0 - jnp.arange(h, dtype=F32)))
    idx = jnp.arange(CHUNK, dtype=F32)
    diff = idx[:, None] - idx[None, :]
    dmat = jnp.where(diff >= 0, jnp.exp(jnp.maximum(diff, 0.0)[None] * log_g[:, None, None]), 0.0)
    dmat = jnp.transpose(dmat, (1, 0, 2)).reshape(CHUNK, h * CHUNK)
    q_w = jnp.exp((idx + 1.0)[:, None] * log_g)
    k_w = jnp.exp((CHUNK - 1.0 - idx)[:, None] * log_g)
    cdec = jnp.exp(CHUNK * log_g)
    return dmat, jnp.repeat(q_w, dv, axis=1), jnp.repeat(k_w, dk, axis=1), jnp.repeat(cdec, dk)[None, :]


def _prologue_kernel(x_ref, g_ref, pos_ref, freq_ref, un_ref, cos_ref, sin_ref):
    un_ref[...] = _rmsnorm(x_ref[...], g_ref[...]).astype(un_ref.dtype)
    ang = pos_ref[...].astype(F32) * freq_ref[...]
    cs = jnp.concatenate([jnp.cos(ang), jnp.sin(ang)], axis=0)
    er = _iota((2 * ROPE_HALF, 2 * LANES), 0)
    ec = _iota((2 * ROPE_HALF, 2 * LANES), 1)
    hit = (ec % ROPE_HALF) == (er % ROPE_HALF)
    sign = jnp.where((ec % (2 * ROPE_HALF)) < ROPE_HALF, -1.0, 1.0)
    spread = jnp.where(hit & (er < ROPE_HALF) & (ec < LANES), 1.0,
                       jnp.where(hit & (er >= ROPE_HALF) & (ec >= LANES), sign, 0.0)).astype(BF16)
    c_hi, c_mid, c_lo = _split3(cs)
    tab = _mm_tn(c_hi, spread) + (_mm_tn(c_mid, spread) + _mm_tn(c_lo, spread))
    cos_ref[...] = tab[:, :LANES]
    sin_ref[...] = tab[:, LANES:]


def _prologue(x, g, positions):
    bsz, s, _ = x.shape
    ts = _time_tile(s, PROLOGUE_TILE_SCALE)
    freq = ROPE_BASE ** (-jnp.arange(ROPE_HALF, dtype=F32) / ROPE_HALF)
    freq = jnp.broadcast_to(freq[:, None], (ROPE_HALF, ts))
    tile = lambda n: pl.BlockSpec((None, ts, n), lambda b, t: (b, t, 0))
    tab = jax.ShapeDtypeStruct((bsz, s, LANES), F32)
    return pl.pallas_call(
        _prologue_kernel,
        grid=(bsz, s // ts),
        in_specs=[tile(D_MODEL), _const_spec((1, D_MODEL)),
                  pl.BlockSpec((None, 1, ts), lambda b, t: (b, 0, t)), _const_spec((ROPE_HALF, ts))],
        out_specs=[tile(D_MODEL), tile(LANES), tile(LANES)],
        out_shape=[jax.ShapeDtypeStruct(x.shape, BF16), tab, tab],
        compiler_params=_compiler_params(2),
        name="norm_rope_prologue",
    )(x, g, positions[:, None, :], freq)


def _ret_branch(un, cos, sin, w, gn_g):
    bsz, s, _ = un.shape
    ts = _time_tile(s, LIGHT_TILE_SCALE)
    dmat, q_w, k_w, cdec = _ret_tables()
    tab = pl.BlockSpec((None, ts, LANES), lambda b, t: (b, t, 0))
    vec = lambda n: _const_spec((1, n))
    return pl.pallas_call(
        _ret_kernel,
        grid=(bsz, s // ts),
        in_specs=[pl.BlockSpec((None, ts, D_MODEL), lambda b, t: (b, t, 0)),
                  tab, tab, _const_spec(w.shape), vec(D_BR),
                  _const_spec(dmat.shape), _const_spec(q_w.shape), _const_spec(k_w.shape), _const_spec(cdec.shape)],
        out_specs=pl.BlockSpec((None, ts, D_BR), lambda b, t: (b, t, 0)),
        out_shape=jax.ShapeDtypeStruct((bsz, s, D_BR), BF16),
        scratch_shapes=[pltpu.VMEM((D_BR, RET_DK), F32),
                        pltpu.VMEM((ts, RET_DK), F32), pltpu.VMEM((ts, RET_DK), F32),
                        pltpu.VMEM((ts, RET_DK), F32), pltpu.VMEM((ts, D_BR), F32),
                        pltpu.VMEM((ts, D_BR), F32)],
        compiler_params=_compiler_params(2),
        name="ret_branch",
    )(un, cos, sin, w, gn_g, dmat, q_w, k_w, cdec)


def _lru_kernel(un_ref, w_ref, cw_ref, cb_ref, wax_ref, bax_ref, lam_ref, o_ref, xbuf, hc_ref):
    ts = un_ref.shape[0]

    @pl.when(pl.program_id(1) == 0)
    def _():
        xbuf[0:CARRY_ROWS, :] = jnp.zeros((CARRY_ROWS, D_BR), F32)
        hc_ref[...] = jnp.zeros(hc_ref.shape, F32)

    z = _mm(un_ref[...], w_ref[...])
    gate = z[:, D_BR:]
    xbuf[CARRY_ROWS:CARRY_ROWS + ts, :] = z[:, :D_BR]
    xc = cb_ref[...] + jnp.zeros((ts, D_BR), F32)
    for j in range(LRU_CONV):
        off = CARRY_ROWS - (LRU_CONV - 1) + j
        xc = xc + xbuf[off:off + ts, :] * cw_ref[j:j + 1, :]
    xbuf[0:CARRY_ROWS, :] = xbuf[ts:ts + CARRY_ROWS, :]

    ri = _sigmoid(_mm(xc, wax_ref[...]) + bax_ref[...])
    log_a = -LRU_C * ri[:, :D_BR] * _softplus(-lam_ref[...])
    a = jnp.exp(log_a)
    u = jnp.sqrt(1.0 - jnp.exp(2.0 * log_a)) * (ri[:, D_BR:] * xc)

    sub = _iota((CARRY_ROWS, D_BR), 0)
    carry = hc_ref[0:1, :]
    blocks = []
    for r0 in range(0, ts, CARRY_ROWS):
        ab = a[r0:r0 + CARRY_ROWS]
        ub = u[r0:r0 + CARRY_ROWS]
        for d in (1, 2, 4):
            keep = sub >= d
            ub = jnp.where(keep, ab * pltpu.roll(ub, d, 0) + ub, ub)
            ab = jnp.where(keep, ab * pltpu.roll(ab, d, 0), ab)
        hb = ub + ab * carry
        carry = hb[CARRY_ROWS - 1:CARRY_ROWS, :]
        blocks.append(hb)
    hh = jnp.concatenate(blocks, axis=0)
    hc_ref[...] = jnp.broadcast_to(hh[ts - 1:ts, :], hc_ref.shape)
    o_ref[...] = (hh * _silu(gate)).astype(o_ref.dtype)


def _lru_branch(un, w, conv_w, conv_b, wax, bax, lam):
    bsz, s, _ = un.shape
    ts = _time_tile(s)
    vec = lambda n: _const_spec((1, n))
    return pl.pallas_call(
        _lru_kernel,
        grid=(bsz, s // ts),
        in_specs=[pl.BlockSpec((None, ts, D_MODEL), lambda b, t: (b, t, 0)),
                  _const_spec(w.shape), _const_spec(conv_w.shape), vec(D_BR),
                  _const_spec(wax.shape), vec(2 * D_BR), vec(D_BR)],
        out_specs=pl.BlockSpec((None, ts, D_BR), lambda b, t: (b, t, 0)),
        out_shape=jax.ShapeDtypeStruct((bsz, s, D_BR), BF16),
        scratch_shapes=[pltpu.VMEM((ts + CARRY_ROWS, D_BR), F32), pltpu.VMEM((CARRY_ROWS, D_BR), F32)],
        compiler_params=_compiler_params(2),
        name="lru_branch",
    )(un, w, conv_w, conv_b, wax, bax, lam)


def _merge_kernel(h_ref, un_ref, y0_ref, y1_ref, y2_ref, y3_ref, gx_ref, wm_ref, wb_ref, wo_ref, o_ref, xn_ref):
    un = un_ref[...]
    merged = None
    for n, y_ref in enumerate((y0_ref, y1_ref, y2_ref, y3_ref)):
        gate = _sigmoid(_mm(un, wm_ref[:, n * D_MODEL:(n + 1) * D_MODEL]))
        term = gate * _mm(y_ref[...], wb_ref[n])
        merged = term if merged is None else merged + term
    out = h_ref[...] + _mm(merged, wo_ref[...])
    o_ref[...] = out
    xn_ref[...] = _rmsnorm(out, gx_ref[...]).astype(xn_ref.dtype)


def _merge(h, un, ys, g_xa, w_merge, w_branch, w_out):
    bsz, s, _ = h.shape
    ts = _time_tile(s)
    tile = lambda n: pl.BlockSpec((None, ts, n), lambda b, t: (b, t, 0))
    return pl.pallas_call(
        _merge_kernel,
        grid=(bsz, s // ts),
        in_specs=[tile(D_MODEL), tile(D_MODEL), tile(D_BR), tile(D_BR), tile(D_BR), tile(D_BR),
                  _const_spec((1, D_MODEL)), _const_spec(w_merge.shape), _const_spec(w_branch.shape),
                  _const_spec(w_out.shape)],
        out_specs=[tile(D_MODEL), tile(D_MODEL)],
        out_shape=[jax.ShapeDtypeStruct(h.shape, F32), jax.ShapeDtypeStruct(h.shape, BF16)],
        compiler_params=_compiler_params(2),
        name="merge_out",
    )(h, un, *ys, g_xa, w_merge, w_branch, w_out)


def _zip_run(*gens):
    live = list(gens)
    while live:
        for gen in list(live):
            if next(gen, StopIteration) is StopIteration:
                live.remove(gen)


def _lru_merge_kernel(h_ref, un_ref, y0_ref, y1_ref, y2_ref, gx_ref, wm_ref, wb_ref, wo_ref,
                      wl_ref, cw_ref, cb_ref, wax_ref, bax_ref, lam_ref, o_ref, xn_ref, xbuf, hc_ref):
    ts = un_ref.shape[0]
    step = 2 * LANES

    @pl.when(pl.program_id(1) == 0)
    def _():
        xbuf[0:CARRY_ROWS, :] = jnp.zeros((CARRY_ROWS, D_BR), F32)
        hc_ref[...] = jnp.zeros(hc_ref.shape, F32)

    un = un_ref[...]
    terms = []
    out = {}

    def merge_matmuls():
        for n, y_ref in enumerate((y0_ref, y1_ref, y2_ref)):
            y = y_ref[...]
            cols = []
            for c0 in range(0, D_MODEL, step):
                gate = _sigmoid(_mm(un, wm_ref[:, n * D_MODEL + c0:n * D_MODEL + c0 + step]))
                yield
                cols.append(gate * _mm(y, wb_ref[n, :, c0:c0 + step]))
                yield
            terms.append(jnp.concatenate(cols, axis=-1))
        cols = []
        for c0 in range(0, D_MODEL, step):
            cols.append(_sigmoid(_mm(un, wm_ref[:, 3 * D_MODEL + c0:3 * D_MODEL + c0 + step])))
            yield
        out["gate3"] = jnp.concatenate(cols, axis=-1)

    def lru():
        z = _mm(un, wl_ref[...])
        yield
        gate = z[:, D_BR:]
        xbuf[CARRY_ROWS:CARRY_ROWS + ts, :] = z[:, :D_BR]
        xc = cb_ref[...] + jnp.zeros((ts, D_BR), F32)
        for j in range(LRU_CONV):
            off = CARRY_ROWS - (LRU_CONV - 1) + j
            xc = xc + xbuf[off:off + ts, :] * cw_ref[j:j + 1, :]
            yield
        xbuf[0:CARRY_ROWS, :] = xbuf[ts:ts + CARRY_ROWS, :]
        ri = _sigmoid(_mm(xc, wax_ref[...]) + bax_ref[...])
        yield
        log_a = -LRU_C * ri[:, :D_BR] * _softplus(-lam_ref[...])
        a = jnp.exp(log_a)
        yield
        u = jnp.sqrt(1.0 - jnp.exp(2.0 * log_a)) * (ri[:, D_BR:] * xc)
        yield
        sub = _iota((CARRY_ROWS, D_BR), 0)
        carry = hc_ref[0:1, :]
        blocks = []
        for r0 in range(0, ts, CARRY_ROWS):
            ab = a[r0:r0 + CARRY_ROWS]
            ub = u[r0:r0 + CARRY_ROWS]
            for d in (1, 2, 4):
                keep = sub >= d
                ub = jnp.where(keep, ab * pltpu.roll(ub, d, 0) + ub, ub)
                ab = jnp.where(keep, ab * pltpu.roll(ab, d, 0), ab)
            hb = ub + ab * carry
            carry = hb[CARRY_ROWS - 1:CARRY_ROWS, :]
            blocks.append(hb)
            if (r0 // CARRY_ROWS) % 4 == 3:
                yield
        hh = jnp.concatenate(blocks, axis=0)
        hc_ref[...] = jnp.broadcast_to(hh[ts - 1:ts, :], hc_ref.shape)
        out["y_lru"] = hh * _silu(gate)

    _zip_run(merge_matmuls(), lru())
    merged = terms[0] + terms[1] + terms[2] + out["gate3"] * _mm(out["y_lru"], wb_ref[3])
    res = h_ref[...] + _mm(merged, wo_ref[...])
    o_ref[...] = res
    xn_ref[...] = _rmsnorm(res, gx_ref[...]).astype(xn_ref.dtype)


def _lru_merge(h, un, ys, g_xa, w_merge, w_branch, w_out, w_lru, conv_w, conv_b, wax, bax, lam):
    bsz, s, _ = h.shape
    ts = _time_tile(s)
    vec = lambda n: _const_spec((1, n))
    tile = lambda n: pl.BlockSpec((None, ts, n), lambda b, t: (b, t, 0))
    return pl.pallas_call(
        _lru_merge_kernel,
        grid=(bsz, s // ts),
        in_specs=[tile(D_MODEL), tile(D_MODEL), tile(D_BR), tile(D_BR), tile(D_BR),
                  vec(D_MODEL), _const_spec(w_merge.shape), _const_spec(w_branch.shape), _const_spec(w_out.shape),
                  _const_spec(w_lru.shape), _const_spec(conv_w.shape), vec(D_BR), _const_spec(wax.shape),
                  vec(2 * D_BR), vec(D_BR)],
        out_specs=[tile(D_MODEL), tile(D_MODEL)],
        out_shape=[jax.ShapeDtypeStruct(h.shape, F32), jax.ShapeDtypeStruct(h.shape, BF16)],
        scratch_shapes=[pltpu.VMEM((ts + CARRY_ROWS, D_BR), F32), pltpu.VMEM((CARRY_ROWS, D_BR), F32)],
        compiler_params=_compiler_params(2),
        name="lru_merge_out",
    )(h, un, *ys, g_xa, w_merge, w_branch, w_out, w_lru, conv_w, conv_b, wax, bax, lam)


def _xattn_kernel(h_ref, xn_ref, mem_ref, gm_ref, wq_ref, wkv_ref, wo_ref, gn_ref, *rest, final_norm):
    if final_norm:
        o_ref, k_ref, v_ref = rest
    else:
        o_ref, un_ref, k_ref, v_ref = rest

    @pl.when(pl.program_id(1) == 0)
    def _():
        mn = _rmsnorm(mem_ref[...], gm_ref[...])
        kv = _mm(mn, wkv_ref[...])
        k_ref[...] = kv[:, :D_MODEL].astype(BF16)
        v_ref[...] = kv[:, D_MODEL:].astype(BF16)

    q = _mm(xn_ref[...], wq_ref[...])
    outs = []
    for hd in range(XA_HEADS):
        ls = slice(hd * XA_HD, (hd + 1) * XA_HD)
        sc = _mm_nt(q[:, ls], k_ref[:, ls]) * (XA_HD ** -0.5)
        sc = sc - jnp.max(sc, axis=-1, keepdims=True)
        e = jnp.exp(sc)
        pr = e / jnp.sum(e, axis=-1, keepdims=True)
        outs.append(_mm(pr, v_ref[:, ls]))
    out = h_ref[...] + _mm(jnp.concatenate(outs, axis=-1), wo_ref[...])
    normed = _rmsnorm(out, gn_ref[...])
    if final_norm:
        o_ref[...] = normed
    else:
        o_ref[...] = out
        un_ref[...] = normed.astype(un_ref.dtype)


def _xattn(h, xn, mem, g_mem, wq, wkv, wo, g_next, final_norm):
    bsz, s, _ = h.shape
    n_mem = mem.shape[1]
    ts = _time_tile(s)
    vec = lambda n: _const_spec((1, n))
    tile = pl.BlockSpec((None, ts, D_MODEL), lambda b, t: (b, t, 0))
    out_f32 = jax.ShapeDtypeStruct(h.shape, F32)
    return pl.pallas_call(
        functools.partial(_xattn_kernel, final_norm=final_norm),
        grid=(bsz, s // ts),
        in_specs=[tile, tile, pl.BlockSpec((None, n_mem, D_MODEL), lambda b, t: (b, 0, 0)),
                  vec(D_MODEL), _const_spec(wq.shape), _const_spec(wkv.shape), _const_spec(wo.shape),
                  vec(D_MODEL)],
        out_specs=tile if final_norm else [tile, tile],
        out_shape=out_f32 if final_norm else [out_f32, jax.ShapeDtypeStruct(h.shape, BF16)],
        scratch_shapes=[pltpu.VMEM((n_mem, D_MODEL), BF16), pltpu.VMEM((n_mem, D_MODEL), BF16)],
        compiler_params=_compiler_params(2),
        name="xattn_final" if final_norm else "xattn",
    )(h, xn, mem, g_mem, wq, wkv, wo, g_next)


def _row(v):
    return v.reshape(1, -1).astype(F32)


def _dense_block_diag(w):
    n, c, d = w.shape
    eye = jnp.eye(n, dtype=w.dtype)
    return (eye[:, None, :, None] * w[:, :, None, :]).reshape(n * c, n * d)


def kernel(x, mem, positions, mix_norm_g, w_in, rw_mu, rw_w0, rw_w2, rw_a0, rw_a2, rw_k_k, rw_k_a, rw_r_k, rw_ln_g, rw_ln_b, gla_f_up, gla_f_b, gla_norm_g, ret_gn_g, lru_conv_w, lru_conv_b, lru_wa, lru_ba, lru_wx, lru_bx, lru_lambda, w_branch, w_out, xa_norm_g, xa_mem_norm_g, xa_wq, xa_wkv, xa_wo, final_norm_g):
    depth = w_in.shape[0]
    n_rw = RW_SHIFT + D_BR
    n_gla = 2 * GLA_DK + 2 * D_BR + GLA_RANK
    n_ret = 2 * RET_DK + 2 * D_BR
    o_gla = n_rw
    o_ret = o_gla + n_gla
    o_lru = o_ret + n_ret
    o_mrg = o_lru + 2 * D_BR
    h = x
    un, cos, sin = _prologue(x, _row(mix_norm_g[0]), positions)
    for l in range(depth):
        wl = w_in[l]
        piece = lambda a, b: wl[:, a:b].astype(BF16)
        last = l == depth - 1

        z64 = jnp.zeros((RW_RANK, D_BR), F32)
        wlr = jnp.concatenate([jnp.concatenate([rw_w2[l], z64], axis=1),
                               jnp.concatenate([z64, rw_a2[l]], axis=1)], axis=0).astype(BF16)
        y_rw = _rwkv_branch(un, piece(0, n_rw), _row(rw_mu[l]), wlr,
                            _row(jnp.concatenate([rw_w0[l], rw_a0[l]])), _row(rw_k_k[l]), _row(rw_k_a[l]),
                            _row(rw_r_k[l]), _row(rw_ln_g[l]), _row(rw_ln_b[l]))

        o_f = o_gla + 2 * GLA_DK + D_BR
        w_gla = jnp.concatenate([wl[:, o_gla:o_f], wl[:, o_f + GLA_RANK:o_ret], wl[:, o_f:o_f + GLA_RANK],
                                 jnp.zeros((D_MODEL, LANES - GLA_RANK), F32)], axis=1).astype(BF16)
        f_up = jnp.concatenate([gla_f_up[l], jnp.zeros((LANES - GLA_RANK, GLA_DK), F32)], axis=0).astype(BF16)
        y_gla = _gla_branch(un, w_gla, f_up, _row(gla_f_b[l]), _row(gla_norm_g[l]))

        y_ret = _ret_branch(un, cos, sin, piece(o_ret, o_lru), _row(ret_gn_g[l]))

        wax = jnp.concatenate([_dense_block_diag(lru_wa[l]), _dense_block_diag(lru_wx[l])], axis=1).astype(BF16)
        h, xn = _lru_merge(h, un, (y_rw, y_gla, y_ret), _row(xa_norm_g[l]), piece(o_mrg, None),
                           w_branch[l].astype(BF16), w_out[l].astype(BF16), piece(o_lru, o_mrg),
                           lru_conv_w[l].astype(F32), _row(lru_conv_b[l]), wax,
                           _row(jnp.concatenate([lru_ba[l], lru_bx[l]])), _row(lru_lambda[l]))
        g_next = _row(final_norm_g if last else mix_norm_g[l + 1])
        res = _xattn(h, xn, mem, _row(xa_mem_norm_g[l]), xa_wq[l].astype(BF16), xa_wkv[l].astype(BF16),
                     xa_wo[l].astype(BF16), g_next, last)
        if last:
            return res
        h, un = res
```

```python
import functools

import jax
import jax.numpy as jnp
from jax import lax
from jax.experimental import pallas as pl
from jax.experimental.pallas import tpu as pltpu

F32 = jnp.float32
BF16 = jnp.bfloat16

D_MODEL = 1024
D_BR = 512
RW_HEADS = 8
RW_HD = D_BR // RW_HEADS
RW_RANK = 64
RW_LN_EPS = 64e-5
RW_SHIFT = 3 * D_BR + 2 * RW_RANK
GLA_DK = D_BR // 2
GLA_RANK = 16
GLA_TAU = 16.0
RET_DK = D_BR // 2
ROPE_BASE = 10000.0
ROPE_HALF = 32
LRU_CONV = 4
LRU_C = 8.0
XA_HEADS = 4
XA_HD = D_MODEL // XA_HEADS
NORM_EPS = 1e-6
CHUNK = 64
LANES = 128
CARRY_ROWS = 8
MAX_TIME_TILE = 512
PROLOGUE_TILE_SCALE = 4
LIGHT_TILE_SCALE = 2
VMEM_LIMIT_BYTES = 56 * 1024 * 1024


def _mm(a, b):
    return jnp.dot(a.astype(BF16), b.astype(BF16), preferred_element_type=F32)


def _mm_nt(a, b):
    return lax.dot_general(a.astype(BF16), b.astype(BF16), (((1,), (1,)), ((), ())),
                           preferred_element_type=F32)


def _mm_tn(a, b):
    return lax.dot_general(a.astype(BF16), b.astype(BF16), (((0,), (0,)), ((), ())),
                           preferred_element_type=F32)


def _split2(x):
    hi = x.astype(BF16)
    lo = (x - hi.astype(F32)).astype(BF16)
    return hi, lo


def _split3(x):
    hi = x.astype(BF16)
    r = x - hi.astype(F32)
    mid = r.astype(BF16)
    lo = (r - mid.astype(F32)).astype(BF16)
    return hi, mid, lo


def _rmsnorm(x, g):
    ms = jnp.mean(x * x, axis=-1, keepdims=True)
    return x * lax.rsqrt(ms + NORM_EPS) * g


def _sigmoid(x):
    return 1.0 / (1.0 + jnp.exp(-x))


def _softplus(x):
    return jnp.maximum(x, 0.0) + jnp.log(1.0 + jnp.exp(-jnp.abs(x)))


def _silu(x):
    return x * _sigmoid(x)


def _iota(shape, dim):
    return lax.broadcasted_iota(jnp.int32, shape, dim)


def _seg_sum(x, width):
    pieces = []
    for j in range(x.shape[-1] // LANES):
        xs = x[:, j * LANES:(j + 1) * LANES]
        if width == LANES:
            s = jnp.sum(xs, axis=-1, keepdims=True)
            pieces.append(jnp.broadcast_to(s, xs.shape))
        else:
            low = _iota(xs.shape, 1) < width
            s_lo = jnp.sum(jnp.where(low, xs, 0.0), axis=-1, keepdims=True)
            s_hi = jnp.sum(jnp.where(low, 0.0, xs), axis=-1, keepdims=True)
            pieces.append(jnp.where(low, s_lo, s_hi))
    return jnp.concatenate(pieces, axis=-1)


def _block_diag(x, n, row_blk, col_blk):
    t = jnp.concatenate([x] * n, axis=0)
    keep = (_iota(t.shape, 0) // row_blk) == (_iota(t.shape, 1) // col_blk)
    return jnp.where(keep, t, jnp.zeros_like(t))


def _block_mask(shape, row_blk, col_blk):
    return (_iota(shape, 0) // row_blk) == (_iota(shape, 1) // col_blk)


def _chunk_cumsum(x):
    ts = x.shape[0]
    tri = jnp.where(_iota((CHUNK, CHUNK), 1) <= _iota((CHUNK, CHUNK), 0), 1.0, 0.0).astype(BF16)
    hi, lo = _split2(x)
    cums = []
    for r0 in range(0, ts, CHUNK):
        cums.append(_mm(tri, hi[r0:r0 + CHUNK]) + _mm(tri, lo[r0:r0 + CHUNK]))
    ends = jnp.concatenate([c[CHUNK - 1:CHUNK] for c in cums], axis=0)
    return cums, ends


def _const_spec(shape):
    nd = len(shape)
    return pl.BlockSpec(shape, lambda *_: (0,) * nd, pipeline_mode=pl.Buffered(1))


def _time_tile(s, scale=1):
    ts = min(MAX_TIME_TILE * scale, s)
    assert s % ts == 0 and ts % CHUNK == 0
    return ts


def _compiler_params(n_axes):
    return pltpu.CompilerParams(dimension_semantics=("arbitrary",) * n_axes,
                                vmem_limit_bytes=VMEM_LIMIT_BYTES)


def _zip(*gens):
    live = list(gens)
    while live:
        for gen in list(live):
            if next(gen, StopIteration) is StopIteration:
                live.remove(gen)
        yield


def _run(gen):
    for _ in gen:
        pass


def _pipeline(stage_fns, n_parts):
    n_stages = len(stage_fns)
    for step in range(n_stages + n_parts - 1):
        yield from _zip(*[stage_fns[step - part](part) for part in range(n_parts) if 0 <= step - part < n_stages])


def _rwkv_body(un_ref, w_ref, mu_ref, wlr_ref, w0a0_ref, kk_ref, ka_ref, rk_ref, lng_ref, lnb_ref,
                 o_ref, zbuf, ht_ref, kt_ref, rt_ref, kd_ref, bd_ref, ke_ref, be_ref, v_ref, dec_ref,
                 wm_ref, u0_ref, akv_ref, ab_ref, y_ref, gate_ref, bonus_ref):
    ts = un_ref.shape[0]
    hw = 4 * RW_HD
    n_parts = 2
    part_rows = ts // n_parts
    part_chunks = part_rows // CHUNK
    n_in = w_ref.shape[1]

    @pl.when(pl.program_id(1) == 0)
    def _():
        zbuf[0:CARRY_ROWS, :] = jnp.zeros((CARRY_ROWS, RW_SHIFT), F32)
        ht_ref[...] = jnp.zeros(ht_ref.shape, F32)

    col = _iota((CHUNK, hw), 1) % CHUNK
    row = _iota((CHUNK, hw), 0)
    strict = col < row
    incl = col <= row
    eye = jnp.where(col == row, 1.0, 0.0)
    blk = _block_mask((hw, hw), RW_HD, RW_HD)
    bdg = functools.partial(_block_diag, n=4, row_blk=CHUNK, col_blk=RW_HD)
    groups = [slice(hw * gi, hw * (gi + 1)) for gi in range(RW_HEADS // 4)]
    zs = {}

    def rows_of(part):
        return slice(part * part_rows, (part + 1) * part_rows)

    def chunks_of(part):
        return [(part * part_chunks + i, slice(part * part_rows + i * CHUNK, part * part_rows + (i + 1) * CHUNK))
                for i in range(part_chunks)]

    def project(part):
        un = un_ref[rows_of(part), :]
        cols = []
        for c0 in range(0, n_in, 2 * LANES):
            cols.append(_mm(un, w_ref[:, c0:min(c0 + 2 * LANES, n_in)]))
            yield
        zs[part] = jnp.concatenate(cols, axis=-1)

    def prepare(part):
        z = zs.pop(part)
        rows = rows_of(part)
        r0 = part * part_rows
        zsh = z[:, :RW_SHIFT]
        gate_ref[rows, :] = z[:, RW_SHIFT:]
        zbuf[CARRY_ROWS + r0:CARRY_ROWS + r0 + part_rows, :] = zsh
        prev = zbuf[CARRY_ROWS - 1 + r0:CARRY_ROWS - 1 + r0 + part_rows, :]
        if part == n_parts - 1:
            zbuf[0:CARRY_ROWS, :] = zbuf[ts:ts + CARRY_ROWS, :]
        sh = zsh + (prev - zsh) * mu_ref[...]
        yield
        r = sh[:, 0:D_BR]
        k = sh[:, D_BR:2 * D_BR]
        v = sh[:, 2 * D_BR:3 * D_BR]
        lr = sh[:, 3 * D_BR:]
        lr = jnp.where(_iota(lr.shape, 1) < RW_RANK, jnp.tanh(lr), lr)
        pre = _mm(lr, wlr_ref[...]) + w0a0_ref[...]
        yield
        log_w = -_softplus(-pre[:, :D_BR]) - 0.5
        lw = -jnp.exp(log_w)
        yield
        a = _sigmoid(pre[:, D_BR:])
        yield
        kk = k * kk_ref[...]
        kk = kk * lax.rsqrt(jnp.maximum(_seg_sum(kk * kk, RW_HD), 1e-24))
        yield
        k_mod = k * (1.0 + (a - 1.0) * ka_ref[...])
        b = kk * a
        bonus_ref[rows, :] = _seg_sum(r * k_mod * rk_ref[...], RW_HD) * v
        v_ref[rows, :] = v
        yield
        cums, ends = _chunk_cumsum(lw)
        yield
        c = jnp.concatenate(cums, axis=0)
        c_end = jnp.concatenate([jnp.broadcast_to(ends[i:i + 1], (CHUNK, D_BR)) for i in range(part_chunks)], axis=0)
        dec_ref[part * part_chunks:(part + 1) * part_chunks, :] = jnp.exp(ends)
        inv = jnp.exp(-c)
        kd_ref[rows, :] = k_mod * inv
        yield
        bd_ref[rows, :] = b * inv
        yield
        to_end = jnp.exp(c_end - c)
        ke_ref[rows, :] = k_mod * to_end
        yield
        be_ref[rows, :] = -(b * to_end)
        yield
        kt_ref[rows, :] = kk * jnp.exp(c - lw)
        yield
        rt_ref[rows, :] = r * jnp.exp(c)
        yield

    def solve(part):
        inst = [(rows, ls) for _, rows in chunks_of(part) for ls in groups]
        kts = [kt_ref[rows, ls] for rows, ls in inst]
        lhs = [jnp.concatenate([kt, rt_ref[rows, ls]], axis=0) for kt, (rows, ls) in zip(kts, inst)]
        sbs = [_mm_nt(x, bdg(bd_ref[rows, ls])) for x, (rows, ls) in zip(lhs, inst)]
        yield
        sks = [_mm_nt(x, bdg(kd_ref[rows, ls])) for x, (rows, ls) in zip(lhs, inst)]
        for sb, (rows, ls) in zip(sbs, inst):
            ab_ref[rows, ls] = jnp.where(incl, sb[CHUNK:], 0.0)
        yield
        lk_ak = [jnp.concatenate([jnp.where(strict, sk[:CHUNK], 0.0), jnp.where(incl, sk[CHUNK:], 0.0)], axis=0)
                 for sk in sks]
        ms = [jnp.where(strict, -sb[:CHUNK], 0.0) for sb in sbs]
        ps = [eye + m for m in ms]
        ms = [_mm(m, bdg(m)) for m in ms]
        yield
        for _ in range(4):
            sq = [_mm(jnp.concatenate([m, p], axis=0), bdg(m)) for m, p in zip(ms, ps)]
            ps = [p + x[CHUNK:] for p, x in zip(ps, sq)]
            ms = [x[:CHUNK] for x in sq]
            yield
        ps = [p + _mm(p, bdg(m)) for m, p in zip(ms, ps)]
        yield
        xs = [_mm(x, bdg(v_ref[rows, ls])) for x, (rows, ls) in zip(lk_ak, inst)]
        yield
        for p, kt, x, (rows, ls) in zip(ps, kts, xs, inst):
            akv_ref[rows, ls] = x[CHUNK:]
            wm_ref[rows, ls] = _mm(p, bdg(kt))
            u0_ref[rows, ls] = _mm(p, bdg(x[:CHUNK]))
        yield

    def scan(part):
        for ci, rows in chunks_of(part):
            hts = [ht_ref[gi] for gi in range(len(groups))]
            wrs = [_mm_nt(jnp.concatenate([wm_ref[rows, ls], rt_ref[rows, ls]], axis=0), ht)
                   for ht, ls in zip(hts, groups)]
            us = [wr[:CHUNK] + u0_ref[rows, ls] for wr, ls in zip(wrs, groups)]
            upds = [_mm_tn(jnp.concatenate([v_ref[rows, ls], u], axis=0),
                           jnp.concatenate([ke_ref[rows, ls], be_ref[rows, ls]], axis=0)) for u, ls in zip(us, groups)]
            for gi, (ht, upd, ls) in enumerate(zip(hts, upds, groups)):
                ht_ref[gi] = ht * dec_ref[ci:ci + 1, ls] + jnp.where(blk, upd, 0.0)
            for wr, u, ls in zip(wrs, us, groups):
                y_ref[rows, ls] = wr[CHUNK:] + (akv_ref[rows, ls] - _mm(ab_ref[rows, ls], bdg(u)))
            yield

    def finish(part):
        for _, rows in chunks_of(part):
            y = y_ref[rows, :]
            mean = _seg_sum(y, RW_HD) * (1.0 / RW_HD)
            yc = y - mean
            var = _seg_sum(yc * yc, RW_HD) * (1.0 / RW_HD)
            y = yc * lax.rsqrt(var + RW_LN_EPS) * lng_ref[...] + lnb_ref[...]
            o_ref[rows, :] = ((y + bonus_ref[rows, :]) * _silu(gate_ref[rows, :])).astype(o_ref.dtype)
            yield

    yield from _pipeline([project, prepare, solve, scan, finish], n_parts)


def _rwkv_kernel(*refs):
    _run(_rwkv_body(*refs))


def _rwkv_branch(un, w, mu, wlr, w0a0, k_k, k_a, r_k, ln_g, ln_b):
    bsz, s, _ = un.shape
    ts = _time_tile(s)
    n_in = w.shape[1]
    vec = lambda n: _const_spec((1, n))
    big = pltpu.VMEM((ts, D_BR), F32)
    return pl.pallas_call(
        _rwkv_kernel,
        grid=(bsz, s // ts),
        in_specs=[pl.BlockSpec((None, ts, D_MODEL), lambda b, t: (b, t, 0)),
                  _const_spec((D_MODEL, n_in)), vec(RW_SHIFT),
                  _const_spec((2 * RW_RANK, 2 * D_BR)), vec(2 * D_BR),
                  vec(D_BR), vec(D_BR), vec(D_BR), vec(D_BR), vec(D_BR)],
        out_specs=pl.BlockSpec((None, ts, D_BR), lambda b, t: (b, t, 0)),
        out_shape=jax.ShapeDtypeStruct((bsz, s, D_BR), BF16),
        scratch_shapes=[pltpu.VMEM((ts + CARRY_ROWS, RW_SHIFT), F32),
                        pltpu.VMEM((RW_HEADS // 4, 4 * RW_HD, 4 * RW_HD), F32),
                        big, big, big, big, big, big, big, pltpu.VMEM((ts // CHUNK, D_BR), F32),
                        big, big, big, big, big, big, big],
        compiler_params=_compiler_params(2),
        name="rwkv_branch",
    )(un, w, mu, wlr, w0a0, k_k, k_a, r_k, ln_g, ln_b)


def _gla_body(un_ref, w_ref, fup_ref, fb_ref, ng_ref, o_ref, st_ref, qd_ref, kd_ref, ke_ref, v_ref, dec_ref, y_ref):
    ts = un_ref.shape[0]
    dk = GLA_DK // 4
    dv = D_BR // 4
    n_parts = 2
    part_rows = ts // n_parts
    part_chunks = part_rows // CHUNK
    n_in = w_ref.shape[1]

    @pl.when(pl.program_id(1) == 0)
    def _():
        st_ref[...] = jnp.zeros(st_ref.shape, F32)

    causal = (_iota((CHUNK, 4 * CHUNK), 1) % CHUNK) <= _iota((CHUNK, 4 * CHUNK), 0)
    st_mask = _block_mask((D_BR, GLA_DK), dv, dk)
    zs, gates, atts, upds = {}, {}, {}, {}

    def rows_of(part):
        return slice(part * part_rows, (part + 1) * part_rows)

    def chunks_of(part):
        return [(part * part_chunks + i, slice(part * part_rows + i * CHUNK, part * part_rows + (i + 1) * CHUNK))
                for i in range(part_chunks)]

    def project(part):
        un = un_ref[rows_of(part), :]
        cols = []
        for c0 in range(0, n_in, 2 * LANES):
            cols.append(_mm(un, w_ref[:, c0:min(c0 + 2 * LANES, n_in)]))
            yield
        zs[part] = jnp.concatenate(cols, axis=-1)

    def decays(part):
        z = zs.pop(part)
        rows = rows_of(part)
        gates[part] = z[:, 2 * GLA_DK + D_BR:2 * GLA_DK + 2 * D_BR]
        f = _mm(z[:, 2 * GLA_DK + 2 * D_BR:], fup_ref[...]) + fb_ref[...]
        yield
        log_f = -_softplus(-f) * (1.0 / GLA_TAU)
        yield
        cums, ends = _chunk_cumsum(log_f)
        yield
        bcum = jnp.concatenate(cums, axis=0)
        b_end = jnp.concatenate([jnp.broadcast_to(ends[i:i + 1], (CHUNK, GLA_DK)) for i in range(part_chunks)], axis=0)
        qd_ref[rows, :] = z[:, 0:GLA_DK] * (dk ** -0.5) * jnp.exp(bcum)
        yield
        kd_ref[rows, :] = z[:, GLA_DK:2 * GLA_DK] * jnp.exp(-bcum)
        yield
        ke_ref[rows, :] = z[:, GLA_DK:2 * GLA_DK] * jnp.exp(b_end - bcum)
        yield
        v_ref[rows, :] = z[:, 2 * GLA_DK:2 * GLA_DK + D_BR]
        dec_ref[part * part_chunks:(part + 1) * part_chunks, :] = jnp.exp(ends)
        yield

    def scores(part):
        atts[part], upds[part] = [], []
        for _, rows in chunks_of(part):
            att = _mm_nt(qd_ref[rows, :], _block_diag(kd_ref[rows, :], 4, CHUNK, dk))
            atts[part].append(jnp.where(causal, att, 0.0))
            yield
            upds[part].append(_mm_tn(v_ref[rows, :], ke_ref[rows, :]))
            yield

    def states(part):
        st = st_ref[...]
        for (ci, rows), att, upd in zip(chunks_of(part), atts.pop(part), upds.pop(part)):
            y_ref[rows, :] = _mm(att, _block_diag(v_ref[rows, :], 4, CHUNK, dv)) + _mm_nt(qd_ref[rows, :], st)
            st = st * dec_ref[ci:ci + 1, :] + jnp.where(st_mask, upd, 0.0)
            yield
        st_ref[...] = st

    def finish(part):
        gate = gates.pop(part)
        for i, (_, rows) in enumerate(chunks_of(part)):
            o = y_ref[rows, :]
            o = o * lax.rsqrt(_seg_sum(o * o, dv) * (1.0 / dv) + NORM_EPS)
            o_ref[rows, :] = (o * ng_ref[...] * _silu(gate[i * CHUNK:(i + 1) * CHUNK])).astype(o_ref.dtype)
            yield

    yield from _pipeline([project, decays, scores, states, finish], n_parts)


def _gla_kernel(*refs):
    _run(_gla_body(*refs))


def _gla_branch(un, w, f_up, f_b, norm_g):
    bsz, s, _ = un.shape
    ts = _time_tile(s)
    vec = lambda n: _const_spec((1, n))
    return pl.pallas_call(
        _gla_kernel,
        grid=(bsz, s // ts),
        in_specs=[pl.BlockSpec((None, ts, D_MODEL), lambda b, t: (b, t, 0)),
                  _const_spec(w.shape), _const_spec(f_up.shape), vec(GLA_DK), vec(D_BR)],
        out_specs=pl.BlockSpec((None, ts, D_BR), lambda b, t: (b, t, 0)),
        out_shape=jax.ShapeDtypeStruct((bsz, s, D_BR), BF16),
        scratch_shapes=[pltpu.VMEM((D_BR, GLA_DK), F32),
                        pltpu.VMEM((ts, GLA_DK), F32), pltpu.VMEM((ts, GLA_DK), F32),
                        pltpu.VMEM((ts, GLA_DK), F32), pltpu.VMEM((ts, D_BR), F32),
                        pltpu.VMEM((ts // CHUNK, GLA_DK), F32), pltpu.VMEM((ts, D_BR), F32)],
        compiler_params=_compiler_params(2),
        name="gla_branch",
    )(un, w, f_up, f_b, norm_g)


def _ret_body(un_ref, cos_ref, sin_ref, w_ref, gn_ref, dmat_ref, qw_ref, kw_ref, cdec_ref,
                o_ref, st_ref, q_ref, k_ref, ke_ref, v_ref, y_ref):
    ts = un_ref.shape[0]
    dk = RET_DK // 4
    dv = D_BR // 4
    n_parts = 2
    part_rows = ts // n_parts
    part_chunks = part_rows // CHUNK
    n_in = w_ref.shape[1]

    @pl.when(pl.program_id(1) == 0)
    def _():
        st_ref[...] = jnp.zeros(st_ref.shape, F32)

    first2 = (_iota((part_rows, RET_DK), 1) % (2 * ROPE_HALF)) < ROPE_HALF
    st_mask = _block_mask((D_BR, RET_DK), dv, dk)
    zs, gates, atts, upds = {}, {}, {}, {}

    def rows_of(part):
        return slice(part * part_rows, (part + 1) * part_rows)

    def chunks_of(part):
        return [slice(part * part_rows + i * CHUNK, part * part_rows + (i + 1) * CHUNK) for i in range(part_chunks)]

    def project(part):
        un = un_ref[rows_of(part), :]
        cols = []
        for c0 in range(0, n_in, 2 * LANES):
            cols.append(_mm(un, w_ref[:, c0:c0 + 2 * LANES]))
            yield
        zs[part] = jnp.concatenate(cols, axis=-1)

    def rotate(part):
        z = zs.pop(part)
        rows = rows_of(part)
        gates[part] = z[:, 2 * RET_DK + D_BR:]
        v_ref[rows, :] = z[:, 2 * RET_DK:2 * RET_DK + D_BR]
        cos2 = jnp.concatenate([cos_ref[rows, :]] * (RET_DK // LANES), axis=-1)
        sin2 = jnp.concatenate([sin_ref[rows, :]] * (RET_DK // LANES), axis=-1)

        def rope(t):
            partner = jnp.where(first2, pltpu.roll(t, RET_DK - ROPE_HALF, 1), pltpu.roll(t, ROPE_HALF, 1))
            return t * cos2 + partner * sin2

        yield
        q_ref[rows, :] = rope(z[:, 0:RET_DK])
        yield
        kr = rope(z[:, RET_DK:2 * RET_DK]) * (dk ** -0.5)
        k_ref[rows, :] = kr
        yield
        ke_ref[rows, :] = kr * jnp.concatenate([kw_ref[...]] * part_chunks, axis=0)
        yield

    def scores(part):
        atts[part], upds[part] = [], []
        for rows in chunks_of(part):
            atts[part].append(_mm_nt(q_ref[rows, :], _block_diag(k_ref[rows, :], 4, CHUNK, dk)) * dmat_ref[...])
            yield
            upds[part].append(_mm_tn(v_ref[rows, :], ke_ref[rows, :]))
            yield

    def states(part):
        st = st_ref[...]
        for rows, att, upd in zip(chunks_of(part), atts.pop(part), upds.pop(part)):
            y_ref[rows, :] = (_mm(att, _block_diag(v_ref[rows, :], 4, CHUNK, dv))
                              + _mm_nt(q_ref[rows, :], st) * qw_ref[...])
            st = st * cdec_ref[...] + jnp.where(st_mask, upd, 0.0)
            yield
        st_ref[...] = st

    def finish(part):
        gate = gates.pop(part)
        for i, rows in enumerate(chunks_of(part)):
            o = y_ref[rows, :]
            mean = _seg_sum(o, dv) * (1.0 / dv)
            oc = o - mean
            var = _seg_sum(oc * oc, dv) * (1.0 / dv)
            o = oc * lax.rsqrt(var + NORM_EPS)
            o_ref[rows, :] = (o * gn_ref[...] * _silu(gate[i * CHUNK:(i + 1) * CHUNK])).astype(o_ref.dtype)
            yield

    yield from _pipeline([project, rotate, scores, states, finish], n_parts)


def _ret_tables():
    h = 4
    dk = RET_DK // h
    dv = D_BR // h
    log_g = jnp.log(1.0 - jnp.exp2(-5.0 - jnp.arange(h, dtype=F32)))
    idx = jnp.arange(CHUNK, dtype=F32)
    diff = idx[:, None] - idx[None, :]
    dmat = jnp.where(diff >= 0, jnp.exp(jnp.maximum(diff, 0.0)[None] * log_g[:, None, None]), 0.0)
    dmat = jnp.transpose(dmat, (1, 0, 2)).reshape(CHUNK, h * CHUNK)
    q_w = jnp.exp((idx + 1.0)[:, None] * log_g)
    k_w = jnp.exp((CHUNK - 1.0 - idx)[:, None] * log_g)
    cdec = jnp.exp(CHUNK * log_g)
    return dmat, jnp.repeat(q_w, dv, axis=1), jnp.repeat(k_w, dk, axis=1), jnp.repeat(cdec, dk)[None, :]


def _prologue_kernel(x_ref, g_ref, pos_ref, freq_ref, un_ref, cos_ref, sin_ref):
    un_ref[...] = _rmsnorm(x_ref[...], g_ref[...]).astype(un_ref.dtype)
    ang = pos_ref[...].astype(F32) * freq_ref[...]
    cs = jnp.concatenate([jnp.cos(ang), jnp.sin(ang)], axis=0)
    er = _iota((2 * ROPE_HALF, 2 * LANES), 0)
    ec = _iota((2 * ROPE_HALF, 2 * LANES), 1)
    hit = (ec % ROPE_HALF) == (er % ROPE_HALF)
    sign = jnp.where((ec % (2 * ROPE_HALF)) < ROPE_HALF, -1.0, 1.0)
    spread = jnp.where(hit & (er < ROPE_HALF) & (ec < LANES), 1.0,
                       jnp.where(hit & (er >= ROPE_HALF) & (ec >= LANES), sign, 0.0)).astype(BF16)
    c_hi, c_mid, c_lo = _split3(cs)
    tab = _mm_tn(c_hi, spread) + (_mm_tn(c_mid, spread) + _mm_tn(c_lo, spread))
    cos_ref[...] = tab[:, :LANES]
    sin_ref[...] = tab[:, LANES:]


def _prologue(x, g, positions):
    bsz, s, _ = x.shape
    ts = _time_tile(s, PROLOGUE_TILE_SCALE)
    freq = ROPE_BASE ** (-jnp.arange(ROPE_HALF, dtype=F32) / ROPE_HALF)
    freq = jnp.broadcast_to(freq[:, None], (ROPE_HALF, ts))
    tile = lambda n: pl.BlockSpec((None, ts, n), lambda b, t: (b, t, 0))
    tab = jax.ShapeDtypeStruct((bsz, s, LANES), F32)
    return pl.pallas_call(
        _prologue_kernel,
        grid=(bsz, s // ts),
        in_specs=[tile(D_MODEL), _const_spec((1, D_MODEL)),
                  pl.BlockSpec((None, 1, ts), lambda b, t: (b, 0, t)), _const_spec((ROPE_HALF, ts))],
        out_specs=[tile(D_MODEL), tile(LANES), tile(LANES)],
        out_shape=[jax.ShapeDtypeStruct(x.shape, BF16), tab, tab],
        compiler_params=_compiler_params(2),
        name="norm_rope_prologue",
    )(x, g, positions[:, None, :], freq)


def _ret_kernel(*refs):
    _run(_ret_body(*refs))


def _ret_branch(un, cos, sin, w, gn_g):
    bsz, s, _ = un.shape
    ts = _time_tile(s, LIGHT_TILE_SCALE)
    dmat, q_w, k_w, cdec = _ret_tables()
    tab = pl.BlockSpec((None, ts, LANES), lambda b, t: (b, t, 0))
    vec = lambda n: _const_spec((1, n))
    return pl.pallas_call(
        _ret_kernel,
        grid=(bsz, s // ts),
        in_specs=[pl.BlockSpec((None, ts, D_MODEL), lambda b, t: (b, t, 0)),
                  tab, tab, _const_spec(w.shape), vec(D_BR),
                  _const_spec(dmat.shape), _const_spec(q_w.shape), _const_spec(k_w.shape), _const_spec(cdec.shape)],
        out_specs=pl.BlockSpec((None, ts, D_BR), lambda b, t: (b, t, 0)),
        out_shape=jax.ShapeDtypeStruct((bsz, s, D_BR), BF16),
        scratch_shapes=[pltpu.VMEM((D_BR, RET_DK), F32),
                        pltpu.VMEM((ts, RET_DK), F32), pltpu.VMEM((ts, RET_DK), F32),
                        pltpu.VMEM((ts, RET_DK), F32), pltpu.VMEM((ts, D_BR), F32),
                        pltpu.VMEM((ts, D_BR), F32)],
        compiler_params=_compiler_params(2),
        name="ret_branch",
    )(un, cos, sin, w, gn_g, dmat, q_w, k_w, cdec)


def _lru_merge_kernel(h_ref, un_ref, y0_ref, y1_ref, y2_ref, gx_ref, wm_ref, wb_ref, wo_ref,
                      wl_ref, cw_ref, cb_ref, wax_ref, bax_ref, lam_ref, o_ref, xn_ref, xbuf, hc_ref):
    ts = un_ref.shape[0]
    step = 2 * LANES

    @pl.when(pl.program_id(1) == 0)
    def _():
        xbuf[0:CARRY_ROWS, :] = jnp.zeros((CARRY_ROWS, D_BR), F32)
        hc_ref[...] = jnp.zeros(hc_ref.shape, F32)

    un = un_ref[...]
    terms = []
    out = {}

    def merge_matmuls():
        for n, y_ref in enumerate((y0_ref, y1_ref, y2_ref)):
            y = y_ref[...]
            cols = []
            for c0 in range(0, D_MODEL, step):
                gate = _sigmoid(_mm(un, wm_ref[:, n * D_MODEL + c0:n * D_MODEL + c0 + step]))
                yield
                cols.append(gate * _mm(y, wb_ref[n, :, c0:c0 + step]))
                yield
            terms.append(jnp.concatenate(cols, axis=-1))
        cols = []
        for c0 in range(0, D_MODEL, step):
            cols.append(_sigmoid(_mm(un, wm_ref[:, 3 * D_MODEL + c0:3 * D_MODEL + c0 + step])))
            yield
        out["gate3"] = jnp.concatenate(cols, axis=-1)

    def lru():
        z = _mm(un, wl_ref[...])
        yield
        gate = z[:, D_BR:]
        xbuf[CARRY_ROWS:CARRY_ROWS + ts, :] = z[:, :D_BR]
        xc = cb_ref[...] + jnp.zeros((ts, D_BR), F32)
        for j in range(LRU_CONV):
            off = CARRY_ROWS - (LRU_CONV - 1) + j
            xc = xc + xbuf[off:off + ts, :] * cw_ref[j:j + 1, :]
            yield
        xbuf[0:CARRY_ROWS, :] = xbuf[ts:ts + CARRY_ROWS, :]
        ri = _sigmoid(_mm(xc, wax_ref[...]) + bax_ref[...])
        yield
        log_a = -LRU_C * ri[:, :D_BR] * _softplus(-lam_ref[...])
        a = jnp.exp(log_a)
        yield
        u = jnp.sqrt(1.0 - jnp.exp(2.0 * log_a)) * (ri[:, D_BR:] * xc)
        yield
        sub = _iota((CARRY_ROWS, D_BR), 0)
        carry = hc_ref[0:1, :]
        blocks = []
        for r0 in range(0, ts, CARRY_ROWS):
            ab = a[r0:r0 + CARRY_ROWS]
            ub = u[r0:r0 + CARRY_ROWS]
            for d in (1, 2, 4):
                keep = sub >= d
                ub = jnp.where(keep, ab * pltpu.roll(ub, d, 0) + ub, ub)
                ab = jnp.where(keep, ab * pltpu.roll(ab, d, 0), ab)
            hb = ub + ab * carry
            carry = hb[CARRY_ROWS - 1:CARRY_ROWS, :]
            blocks.append(hb)
            if (r0 // CARRY_ROWS) % 4 == 3:
                yield
        hh = jnp.concatenate(blocks, axis=0)
        hc_ref[...] = jnp.broadcast_to(hh[ts - 1:ts, :], hc_ref.shape)
        out["y_lru"] = hh * _silu(gate)

    _run(_zip(merge_matmuls(), lru()))
    merged = terms[0] + terms[1] + terms[2] + out["gate3"] * _mm(out["y_lru"], wb_ref[3])
    res = h_ref[...] + _mm(merged, wo_ref[...])
    o_ref[...] = res
    xn_ref[...] = _rmsnorm(res, gx_ref[...]).astype(xn_ref.dtype)


def _lru_merge(h, un, ys, g_xa, w_merge, w_branch, w_out, w_lru, conv_w, conv_b, wax, bax, lam):
    bsz, s, _ = h.shape
    ts = _time_tile(s)
    vec = lambda n: _const_spec((1, n))
    tile = lambda n: pl.BlockSpec((None, ts, n), lambda b, t: (b, t, 0))
    return pl.pallas_call(
        _lru_merge_kernel,
        grid=(bsz, s // ts),
        in_specs=[tile(D_MODEL), tile(D_MODEL), tile(D_BR), tile(D_BR), tile(D_BR),
                  vec(D_MODEL), _const_spec(w_merge.shape), _const_spec(w_branch.shape), _const_spec(w_out.shape),
                  _const_spec(w_lru.shape), _const_spec(conv_w.shape), vec(D_BR), _const_spec(wax.shape),
                  vec(2 * D_BR), vec(D_BR)],
        out_specs=[tile(D_MODEL), tile(D_MODEL)],
        out_shape=[jax.ShapeDtypeStruct(h.shape, F32), jax.ShapeDtypeStruct(h.shape, BF16)],
        scratch_shapes=[pltpu.VMEM((ts + CARRY_ROWS, D_BR), F32), pltpu.VMEM((CARRY_ROWS, D_BR), F32)],
        compiler_params=_compiler_params(2),
        name="lru_merge_out",
    )(h, un, *ys, g_xa, w_merge, w_branch, w_out, w_lru, conv_w, conv_b, wax, bax, lam)


def _xattn_kernel(h_ref, xn_ref, mem_ref, gm_ref, wq_ref, wkv_ref, wo_ref, gn_ref, *rest, final_norm):
    if final_norm:
        o_ref, k_ref, v_ref = rest
    else:
        o_ref, un_ref, k_ref, v_ref = rest

    @pl.when(pl.program_id(1) == 0)
    def _():
        mn = _rmsnorm(mem_ref[...], gm_ref[...])
        kv = _mm(mn, wkv_ref[...])
        k_ref[...] = kv[:, :D_MODEL].astype(BF16)
        v_ref[...] = kv[:, D_MODEL:].astype(BF16)

    q = _mm(xn_ref[...], wq_ref[...])
    outs = []
    for hd in range(XA_HEADS):
        ls = slice(hd * XA_HD, (hd + 1) * XA_HD)
        sc = _mm_nt(q[:, ls], k_ref[:, ls]) * (XA_HD ** -0.5)
        sc = sc - jnp.max(sc, axis=-1, keepdims=True)
        e = jnp.exp(sc)
        pr = e / jnp.sum(e, axis=-1, keepdims=True)
        outs.append(_mm(pr, v_ref[:, ls]))
    out = h_ref[...] + _mm(jnp.concatenate(outs, axis=-1), wo_ref[...])
    normed = _rmsnorm(out, gn_ref[...])
    if final_norm:
        o_ref[...] = normed
    else:
        o_ref[...] = out
        un_ref[...] = normed.astype(un_ref.dtype)


def _xattn(h, xn, mem, g_mem, wq, wkv, wo, g_next, final_norm):
    bsz, s, _ = h.shape
    n_mem = mem.shape[1]
    ts = _time_tile(s)
    vec = lambda n: _const_spec((1, n))
    tile = pl.BlockSpec((None, ts, D_MODEL), lambda b, t: (b, t, 0))
    out_f32 = jax.ShapeDtypeStruct(h.shape, F32)
    return pl.pallas_call(
        functools.partial(_xattn_kernel, final_norm=final_norm),
        grid=(bsz, s // ts),
        in_specs=[tile, tile, pl.BlockSpec((None, n_mem, D_MODEL), lambda b, t: (b, 0, 0)),
                  vec(D_MODEL), _const_spec(wq.shape), _const_spec(wkv.shape), _const_spec(wo.shape),
                  vec(D_MODEL)],
        out_specs=tile if final_norm else [tile, tile],
        out_shape=out_f32 if final_norm else [out_f32, jax.ShapeDtypeStruct(h.shape, BF16)],
        scratch_shapes=[pltpu.VMEM((n_mem, D_MODEL), BF16), pltpu.VMEM((n_mem, D_MODEL), BF16)],
        compiler_params=_compiler_params(2),
        name="xattn_final" if final_norm else "xattn",
    )(h, xn, mem, g_mem, wq, wkv, wo, g_next)


def _row(v):
    return v.reshape(1, -1).astype(F32)


def _dense_block_diag(w):
    n, c, d = w.shape
    eye = jnp.eye(n, dtype=w.dtype)
    return (eye[:, None, :, None] * w[:, :, None, :]).reshape(n * c, n * d)


def kernel(x, mem, positions, mix_norm_g, w_in, rw_mu, rw_w0, rw_w2, rw_a0, rw_a2, rw_k_k, rw_k_a, rw_r_k, rw_ln_g, rw_ln_b, gla_f_up, gla_f_b, gla_norm_g, ret_gn_g, lru_conv_w, lru_conv_b, lru_wa, lru_ba, lru_wx, lru_bx, lru_lambda, w_branch, w_out, xa_norm_g, xa_mem_norm_g, xa_wq, xa_wkv, xa_wo, final_norm_g):
    depth = w_in.shape[0]
    n_rw = RW_SHIFT + D_BR
    n_gla = 2 * GLA_DK + 2 * D_BR + GLA_RANK
    n_ret = 2 * RET_DK + 2 * D_BR
    o_gla = n_rw
    o_ret = o_gla + n_gla
    o_lru = o_ret + n_ret
    o_mrg = o_lru + 2 * D_BR
    h = x
    un, cos, sin = _prologue(x, _row(mix_norm_g[0]), positions)
    for l in range(depth):
        wl = w_in[l]
        piece = lambda a, b: wl[:, a:b].astype(BF16)
        last = l == depth - 1

        z64 = jnp.zeros((RW_RANK, D_BR), F32)
        wlr = jnp.concatenate([jnp.concatenate([rw_w2[l], z64], axis=1),
                               jnp.concatenate([z64, rw_a2[l]], axis=1)], axis=0).astype(BF16)
        y_rw = _rwkv_branch(un, piece(0, n_rw), _row(rw_mu[l]), wlr,
                            _row(jnp.concatenate([rw_w0[l], rw_a0[l]])), _row(rw_k_k[l]), _row(rw_k_a[l]),
                            _row(rw_r_k[l]), _row(rw_ln_g[l]), _row(rw_ln_b[l]))

        o_f = o_gla + 2 * GLA_DK + D_BR
        w_gla = jnp.concatenate([wl[:, o_gla:o_f], wl[:, o_f + GLA_RANK:o_ret], wl[:, o_f:o_f + GLA_RANK],
                                 jnp.zeros((D_MODEL, LANES - GLA_RANK), F32)], axis=1).astype(BF16)
        f_up = jnp.concatenate([gla_f_up[l], jnp.zeros((LANES - GLA_RANK, GLA_DK), F32)], axis=0).astype(BF16)
        y_gla = _gla_branch(un, w_gla, f_up, _row(gla_f_b[l]), _row(gla_norm_g[l]))

        y_ret = _ret_branch(un, cos, sin, piece(o_ret, o_lru), _row(ret_gn_g[l]))

        wax = jnp.concatenate([_dense_block_diag(lru_wa[l]), _dense_block_diag(lru_wx[l])], axis=1).astype(BF16)
        h, xn = _lru_merge(h, un, (y_rw, y_gla, y_ret), _row(xa_norm_g[l]), piece(o_mrg, None),
                           w_branch[l].astype(BF16), w_out[l].astype(BF16), piece(o_lru, o_mrg),
                           lru_conv_w[l].astype(F32), _row(lru_conv_b[l]), wax,
                           _row(jnp.concatenate([lru_ba[l], lru_bx[l]])), _row(lru_lambda[l]))
        g_next = _row(final_norm_g if last else mix_norm_g[l + 1])
        res = _xattn(h, xn, mem, _row(xa_mem_norm_g[l]), xa_wq[l].astype(BF16), xa_wkv[l].astype(BF16),
                     xa_wo[l].astype(BF16), g_next, last)
        if last:
            return res
        h, un = res
```

```python
import functools

import jax
import jax.numpy as jnp
from jax import lax
from jax.experimental import pallas as pl
from jax.experimental.pallas import tpu as pltpu

F32 = jnp.float32
BF16 = jnp.bfloat16

D_MODEL = 1024
D_BR = 512
RW_HEADS = 8
RW_HD = D_BR // RW_HEADS
RW_RANK = 64
RW_LN_EPS = 64e-5
RW_SHIFT = 3 * D_BR + 2 * RW_RANK
GLA_DK = D_BR // 2
GLA_RANK = 16
GLA_TAU = 16.0
RET_DK = D_BR // 2
ROPE_BASE = 10000.0
ROPE_HALF = 32
LRU_CONV = 4
LRU_C = 8.0
XA_HEADS = 4
XA_HD = D_MODEL // XA_HEADS
NORM_EPS = 1e-6
CHUNK = 64
LANES = 128
CARRY_ROWS = 8
MAX_TIME_TILE = 512
PROLOGUE_TILE_SCALE = 4
LIGHT_TILE_SCALE = 2
VMEM_LIMIT_BYTES = 56 * 1024 * 1024
ROW_PAD_LANES = 512


def _mm(a, b):
    return jnp.dot(a.astype(BF16), b.astype(BF16), preferred_element_type=F32)


def _mm_nt(a, b):
    return lax.dot_general(a.astype(BF16), b.astype(BF16), (((1,), (1,)), ((), ())),
                           preferred_element_type=F32)


def _mm_tn(a, b):
    return lax.dot_general(a.astype(BF16), b.astype(BF16), (((0,), (0,)), ((), ())),
                           preferred_element_type=F32)


def _split2(x):
    hi = x.astype(BF16)
    lo = (x - hi.astype(F32)).astype(BF16)
    return hi, lo


def _split3(x):
    hi = x.astype(BF16)
    r = x - hi.astype(F32)
    mid = r.astype(BF16)
    lo = (r - mid.astype(F32)).astype(BF16)
    return hi, mid, lo


def _rmsnorm(x, g):
    ms = jnp.mean(x * x, axis=-1, keepdims=True)
    return x * lax.rsqrt(ms + NORM_EPS) * g


def _sigmoid(x):
    return 1.0 / (1.0 + jnp.exp(-x))


def _softplus(x):
    return jnp.maximum(x, 0.0) + jnp.log(1.0 + jnp.exp(-jnp.abs(x)))


def _silu(x):
    return x * _sigmoid(x)


def _iota(shape, dim):
    return lax.broadcasted_iota(jnp.int32, shape, dim)


def _seg_sum(x, width):
    pieces = []
    for j in range(x.shape[-1] // LANES):
        xs = x[:, j * LANES:(j + 1) * LANES]
        if width == LANES:
            s = jnp.sum(xs, axis=-1, keepdims=True)
            pieces.append(jnp.broadcast_to(s, xs.shape))
        else:
            low = _iota(xs.shape, 1) < width
            s_lo = jnp.sum(jnp.where(low, xs, 0.0), axis=-1, keepdims=True)
            s_hi = jnp.sum(jnp.where(low, 0.0, xs), axis=-1, keepdims=True)
            pieces.append(jnp.where(low, s_lo, s_hi))
    return jnp.concatenate(pieces, axis=-1)


def _block_diag(x, n, row_blk, col_blk):
    t = jnp.concatenate([x] * n, axis=0)
    keep = (_iota(t.shape, 0) // row_blk) == (_iota(t.shape, 1) // col_blk)
    return jnp.where(keep, t, jnp.zeros_like(t))


def _block_mask(shape, row_blk, col_blk):
    return (_iota(shape, 0) // row_blk) == (_iota(shape, 1) // col_blk)


def _chunk_cumsum(x):
    ts = x.shape[0]
    tri = jnp.where(_iota((CHUNK, CHUNK), 1) <= _iota((CHUNK, CHUNK), 0), 1.0, 0.0).astype(BF16)
    hi, lo = _split2(x)
    cums = []
    for r0 in range(0, ts, CHUNK):
        cums.append(_mm(tri, hi[r0:r0 + CHUNK]) + _mm(tri, lo[r0:r0 + CHUNK]))
    ends = jnp.concatenate([c[CHUNK - 1:CHUNK] for c in cums], axis=0)
    return cums, ends


def _const_spec(shape):
    nd = len(shape)
    return pl.BlockSpec(shape, lambda *_: (0,) * nd, pipeline_mode=pl.Buffered(1))


def _time_tile(s, scale=1):
    ts = min(MAX_TIME_TILE * scale, s)
    assert s % ts == 0 and ts % CHUNK == 0
    return ts


def _compiler_params(n_axes):
    return pltpu.CompilerParams(dimension_semantics=("arbitrary",) * n_axes,
                                vmem_limit_bytes=VMEM_LIMIT_BYTES)


def _zip(*gens):
    live = list(gens)
    while live:
        for gen in list(live):
            if next(gen, StopIteration) is StopIteration:
                live.remove(gen)
        yield


def _run(gen):
    for _ in gen:
        pass


def _pipeline(stage_fns, n_parts):
    n_stages = len(stage_fns)
    for step in range(n_stages + n_parts - 1):
        yield from _zip(*[stage_fns[step - part](part) for part in range(n_parts) if 0 <= step - part < n_stages])


def _rwkv_body(un_ref, w_ref, mu_ref, wlr_ref, w0a0_ref, kk_ref, ka_ref, rk_ref, lng_ref, lnb_ref,
                 o_ref, ht_ref, kt_ref, rt_ref, kd_ref, bd_ref, ke_ref, be_ref, v_ref, dec_ref,
                 wm_ref, u0_ref, akv_ref, ab_ref, y_ref, gate_ref, bonus_ref, zbuf):
    ts = un_ref.shape[0]
    hw = 4 * RW_HD
    n_parts = 2
    part_rows = ts // n_parts
    part_chunks = part_rows // CHUNK
    n_in = w_ref.shape[1]

    @pl.when(pl.program_id(1) == 0)
    def _():
        zbuf[0:CARRY_ROWS, :] = jnp.zeros((CARRY_ROWS, RW_SHIFT), F32)
        ht_ref[...] = jnp.zeros(ht_ref.shape, F32)

    col = _iota((CHUNK, hw), 1) % CHUNK
    row = _iota((CHUNK, hw), 0)
    strict = col < row
    incl = col <= row
    eye = jnp.where(col == row, 1.0, 0.0)
    blk = _block_mask((hw, hw), RW_HD, RW_HD)
    bdg = functools.partial(_block_diag, n=4, row_blk=CHUNK, col_blk=RW_HD)
    groups = [slice(hw * gi, hw * (gi + 1)) for gi in range(RW_HEADS // 4)]
    zs = {}

    def rows_of(part):
        return slice(part * part_rows, (part + 1) * part_rows)

    def chunks_of(part):
        return [(part * part_chunks + i, slice(part * part_rows + i * CHUNK, part * part_rows + (i + 1) * CHUNK))
                for i in range(part_chunks)]

    def project(part):
        un = un_ref[rows_of(part), :]
        cols = []
        for c0 in range(0, n_in, 2 * LANES):
            cols.append(_mm(un, w_ref[:, c0:min(c0 + 2 * LANES, n_in)]))
            yield
        zs[part] = jnp.concatenate(cols, axis=-1)

    def prepare(part):
        z = zs.pop(part)
        rows = rows_of(part)
        r0 = part * part_rows
        zsh = z[:, :RW_SHIFT]
        gate_ref[rows, :] = z[:, RW_SHIFT:]
        zbuf[CARRY_ROWS + r0:CARRY_ROWS + r0 + part_rows, :] = zsh
        prev = zbuf[CARRY_ROWS - 1 + r0:CARRY_ROWS - 1 + r0 + part_rows, :]
        if part == n_parts - 1:
            zbuf[0:CARRY_ROWS, :] = zbuf[ts:ts + CARRY_ROWS, :]
        sh = zsh + (prev - zsh) * mu_ref[:, :RW_SHIFT]
        yield
        r = sh[:, 0:D_BR]
        k = sh[:, D_BR:2 * D_BR]
        v = sh[:, 2 * D_BR:3 * D_BR]
        lr = sh[:, 3 * D_BR:]
        lr = jnp.where(_iota(lr.shape, 1) < RW_RANK, jnp.tanh(lr), lr)
        pre = _mm(lr, wlr_ref[...]) + w0a0_ref[...]
        yield
        log_w = -_softplus(-pre[:, :D_BR]) - 0.5
        lw = -jnp.exp(log_w)
        yield
        a = _sigmoid(pre[:, D_BR:])
        yield
        kk = k * kk_ref[...]
        kk = kk * lax.rsqrt(jnp.maximum(_seg_sum(kk * kk, RW_HD), 1e-24))
        yield
        k_mod = k * (1.0 + (a - 1.0) * ka_ref[...])
        b = kk * a
        bonus_ref[rows, :] = _seg_sum(r * k_mod * rk_ref[...], RW_HD) * v
        v_ref[rows, :] = v
        yield
        cums, ends = _chunk_cumsum(lw)
        yield
        c = jnp.concatenate(cums, axis=0)
        c_end = jnp.concatenate([jnp.broadcast_to(ends[i:i + 1], (CHUNK, D_BR)) for i in range(part_chunks)], axis=0)
        dec_ref[part * part_chunks:(part + 1) * part_chunks, :] = jnp.exp(ends)
        inv = jnp.exp(-c)
        kd_ref[rows, :] = k_mod * inv
        yield
        bd_ref[rows, :] = b * inv
        yield
        to_end = jnp.exp(c_end - c)
        ke_ref[rows, :] = k_mod * to_end
        yield
        be_ref[rows, :] = -(b * to_end)
        yield
        kt_ref[rows, :] = kk * jnp.exp(c - lw)
        yield
        rt_ref[rows, :] = r * jnp.exp(c)
        yield

    def solve(part):
        inst = [(rows, ls) for _, rows in chunks_of(part) for ls in groups]
        kts = [kt_ref[rows, ls] for rows, ls in inst]
        lhs = [jnp.concatenate([kt, rt_ref[rows, ls]], axis=0) for kt, (rows, ls) in zip(kts, inst)]
        sbs = [_mm_nt(x, bdg(bd_ref[rows, ls])) for x, (rows, ls) in zip(lhs, inst)]
        yield
        sks = [_mm_nt(x, bdg(kd_ref[rows, ls])) for x, (rows, ls) in zip(lhs, inst)]
        for sb, (rows, ls) in zip(sbs, inst):
            ab_ref[rows, ls] = jnp.where(incl, sb[CHUNK:], 0.0)
        yield
        lk_ak = [jnp.concatenate([jnp.where(strict, sk[:CHUNK], 0.0), jnp.where(incl, sk[CHUNK:], 0.0)], axis=0)
                 for sk in sks]
        ms = [jnp.where(strict, -sb[:CHUNK], 0.0) for sb in sbs]
        ps = [eye + m for m in ms]
        ms = [_mm(m, bdg(m)) for m in ms]
        yield
        for _ in range(4):
            sq = [_mm(jnp.concatenate([m, p], axis=0), bdg(m)) for m, p in zip(ms, ps)]
            ps = [p + x[CHUNK:] for p, x in zip(ps, sq)]
            ms = [x[:CHUNK] for x in sq]
            yield
        ps = [p + _mm(p, bdg(m)) for m, p in zip(ms, ps)]
        yield
        xs = [_mm(x, bdg(v_ref[rows, ls])) for x, (rows, ls) in zip(lk_ak, inst)]
        yield
        for p, kt, x, (rows, ls) in zip(ps, kts, xs, inst):
            akv_ref[rows, ls] = x[CHUNK:]
            wm_ref[rows, ls] = _mm(p, bdg(kt))
            u0_ref[rows, ls] = _mm(p, bdg(x[:CHUNK]))
        yield

    def scan(part):
        for ci, rows in chunks_of(part):
            hts = [ht_ref[gi] for gi in range(len(groups))]
            wrs = [_mm_nt(jnp.concatenate([wm_ref[rows, ls], rt_ref[rows, ls]], axis=0), ht)
                   for ht, ls in zip(hts, groups)]
            us = [wr[:CHUNK] + u0_ref[rows, ls] for wr, ls in zip(wrs, groups)]
            upds = [_mm_tn(jnp.concatenate([v_ref[rows, ls], u], axis=0),
                           jnp.concatenate([ke_ref[rows, ls], be_ref[rows, ls]], axis=0)) for u, ls in zip(us, groups)]
            for gi, (ht, upd, ls) in enumerate(zip(hts, upds, groups)):
                ht_ref[gi] = ht * dec_ref[ci:ci + 1, ls] + jnp.where(blk, upd, 0.0)
            for wr, u, ls in zip(wrs, us, groups):
                y_ref[rows, ls] = wr[CHUNK:] + (akv_ref[rows, ls] - _mm(ab_ref[rows, ls], bdg(u)))
            yield

    def finish(part):
        for _, rows in chunks_of(part):
            y = y_ref[rows, :]
            mean = _seg_sum(y, RW_HD) * (1.0 / RW_HD)
            yc = y - mean
            var = _seg_sum(yc * yc, RW_HD) * (1.0 / RW_HD)
            y = yc * lax.rsqrt(var + RW_LN_EPS) * lng_ref[...] + lnb_ref[...]
            o_ref[rows, :] = ((y + bonus_ref[rows, :]) * _silu(gate_ref[rows, :])).astype(o_ref.dtype)
            yield

    yield from _pipeline([project, prepare, solve, scan, finish], n_parts)


def _rwkv_kernel(*refs):
    _run(_rwkv_body(*refs))


def _rwkv_branch(un, w, mu, wlr, w0a0, k_k, k_a, r_k, ln_g, ln_b):
    bsz, s, _ = un.shape
    ts = _time_tile(s, LIGHT_TILE_SCALE)
    n_in = w.shape[1]
    vec = lambda n: _const_spec((1, n))
    big = pltpu.VMEM((ts, D_BR), F32)
    return pl.pallas_call(
        _rwkv_kernel,
        grid=(bsz, s // ts),
        in_specs=[pl.BlockSpec((None, ts, D_MODEL), lambda b, t: (b, t, 0)),
                  _const_spec((D_MODEL, n_in)), vec(mu.shape[1]),
                  _const_spec((2 * RW_RANK, 2 * D_BR)), vec(2 * D_BR),
                  vec(D_BR), vec(D_BR), vec(D_BR), vec(D_BR), vec(D_BR)],
        out_specs=pl.BlockSpec((None, ts, D_BR), lambda b, t: (b, t, 0)),
        out_shape=jax.ShapeDtypeStruct((bsz, s, D_BR), BF16),
        scratch_shapes=[pltpu.VMEM((RW_HEADS // 4, 4 * RW_HD, 4 * RW_HD), F32),
                        big, big, big, big, big, big, big, pltpu.VMEM((ts // CHUNK, D_BR), F32),
                        big, big, big, big, big, big, big,
                        pltpu.VMEM((ts + CARRY_ROWS, RW_SHIFT), F32)],
        compiler_params=_compiler_params(2),
        name="rwkv_branch",
    )(un, w, mu, wlr, w0a0, k_k, k_a, r_k, ln_g, ln_b)


def _gla_body(un_ref, w_ref, fup_ref, fb_ref, ng_ref, o_ref, st_ref, qd_ref, kd_ref, ke_ref, v_ref, y_ref, dec_ref):
    ts = un_ref.shape[0]
    dk = GLA_DK // 4
    dv = D_BR // 4
    n_parts = 2
    part_rows = ts // n_parts
    part_chunks = part_rows // CHUNK
    n_in = w_ref.shape[1]

    @pl.when(pl.program_id(1) == 0)
    def _():
        st_ref[...] = jnp.zeros(st_ref.shape, F32)

    causal = (_iota((CHUNK, 4 * CHUNK), 1) % CHUNK) <= _iota((CHUNK, 4 * CHUNK), 0)
    st_mask = _block_mask((D_BR, GLA_DK), dv, dk)
    zs, gates, atts, upds = {}, {}, {}, {}

    def rows_of(part):
        return slice(part * part_rows, (part + 1) * part_rows)

    def chunks_of(part):
        return [(part * part_chunks + i, slice(part * part_rows + i * CHUNK, part * part_rows + (i + 1) * CHUNK))
                for i in range(part_chunks)]

    def project(part):
        un = un_ref[rows_of(part), :]
        cols = []
        for c0 in range(0, n_in, 2 * LANES):
            cols.append(_mm(un, w_ref[:, c0:min(c0 + 2 * LANES, n_in)]))
            yield
        zs[part] = jnp.concatenate(cols, axis=-1)

    def decays(part):
        z = zs.pop(part)
        rows = rows_of(part)
        gates[part] = z[:, 2 * GLA_DK + D_BR:2 * GLA_DK + 2 * D_BR]
        f = _mm(z[:, 2 * GLA_DK + 2 * D_BR:], fup_ref[...]) + fb_ref[:, :GLA_DK]
        yield
        log_f = -_softplus(-f) * (1.0 / GLA_TAU)
        yield
        cums, ends = _chunk_cumsum(log_f)
        yield
        bcum = jnp.concatenate(cums, axis=0)
        b_end = jnp.concatenate([jnp.broadcast_to(ends[i:i + 1], (CHUNK, GLA_DK)) for i in range(part_chunks)], axis=0)
        qd_ref[rows, :] = z[:, 0:GLA_DK] * (dk ** -0.5) * jnp.exp(bcum)
        yield
        kd_ref[rows, :] = z[:, GLA_DK:2 * GLA_DK] * jnp.exp(-bcum)
        yield
        ke_ref[rows, :] = z[:, GLA_DK:2 * GLA_DK] * jnp.exp(b_end - bcum)
        yield
        v_ref[rows, :] = z[:, 2 * GLA_DK:2 * GLA_DK + D_BR]
        dec_ref[part * part_chunks:(part + 1) * part_chunks, :] = jnp.exp(ends)
        yield

    def scores(part):
        atts[part], upds[part] = [], []
        for _, rows in chunks_of(part):
            att = _mm_nt(qd_ref[rows, :], _block_diag(kd_ref[rows, :], 4, CHUNK, dk))
            atts[part].append(jnp.where(causal, att, 0.0))
            yield
            upds[part].append(_mm_tn(v_ref[rows, :], ke_ref[rows, :]))
            yield

    def states(part):
        st = st_ref[...]
        for (ci, rows), att, upd in zip(chunks_of(part), atts.pop(part), upds.pop(part)):
            y_ref[rows, :] = _mm(att, _block_diag(v_ref[rows, :], 4, CHUNK, dv)) + _mm_nt(qd_ref[rows, :], st)
            st = st * dec_ref[ci:ci + 1, :] + jnp.where(st_mask, upd, 0.0)
            yield
        st_ref[...] = st

    def finish(part):
        gate = gates.pop(part)
        for i, (_, rows) in enumerate(chunks_of(part)):
            o = y_ref[rows, :]
            o = o * lax.rsqrt(_seg_sum(o * o, dv) * (1.0 / dv) + NORM_EPS)
            o_ref[rows, :] = (o * ng_ref[...] * _silu(gate[i * CHUNK:(i + 1) * CHUNK])).astype(o_ref.dtype)
            yield

    yield from _pipeline([project, decays, scores, states, finish], n_parts)


def _gla_kernel(*refs):
    _run(_gla_body(*refs))


def _gla_branch(un, w, f_up, f_b, norm_g):
    bsz, s, _ = un.shape
    ts = _time_tile(s)
    vec = lambda n: _const_spec((1, n))
    return pl.pallas_call(
        _gla_kernel,
        grid=(bsz, s // ts),
        in_specs=[pl.BlockSpec((None, ts, D_MODEL), lambda b, t: (b, t, 0)),
                  _const_spec(w.shape), _const_spec(f_up.shape), vec(f_b.shape[1]), vec(D_BR)],
        out_specs=pl.BlockSpec((None, ts, D_BR), lambda b, t: (b, t, 0)),
        out_shape=jax.ShapeDtypeStruct((bsz, s, D_BR), BF16),
        scratch_shapes=[pltpu.VMEM((D_BR, GLA_DK), F32),
                        pltpu.VMEM((ts, GLA_DK), F32), pltpu.VMEM((ts, GLA_DK), F32),
                        pltpu.VMEM((ts, GLA_DK), F32), pltpu.VMEM((ts, D_BR), F32),
                        pltpu.VMEM((ts, D_BR), F32), pltpu.VMEM((ts // CHUNK, GLA_DK), F32)],
        compiler_params=_compiler_params(2),
        name="gla_branch",
    )(un, w, f_up, f_b, norm_g)


def _ret_body(un_ref, cos_ref, sin_ref, w_ref, gn_ref, dmat_ref, qw_ref, kw_ref, cdec_ref,
                o_ref, st_ref, q_ref, k_ref, ke_ref, v_ref, y_ref):
    ts = un_ref.shape[0]
    dk = RET_DK // 4
    dv = D_BR // 4
    n_parts = 2
    part_rows = ts // n_parts
    part_chunks = part_rows // CHUNK
    n_in = w_ref.shape[1]

    @pl.when(pl.program_id(1) == 0)
    def _():
        st_ref[...] = jnp.zeros(st_ref.shape, F32)

    first2 = (_iota((part_rows, RET_DK), 1) % (2 * ROPE_HALF)) < ROPE_HALF
    st_mask = _block_mask((D_BR, RET_DK), dv, dk)
    zs, gates, atts, upds = {}, {}, {}, {}

    def rows_of(part):
        return slice(part * part_rows, (part + 1) * part_rows)

    def chunks_of(part):
        return [slice(part * part_rows + i * CHUNK, part * part_rows + (i + 1) * CHUNK) for i in range(part_chunks)]

    def project(part):
        un = un_ref[rows_of(part), :]
        cols = []
        for c0 in range(0, n_in, 2 * LANES):
            cols.append(_mm(un, w_ref[:, c0:c0 + 2 * LANES]))
            yield
        zs[part] = jnp.concatenate(cols, axis=-1)

    def rotate(part):
        z = zs.pop(part)
        rows = rows_of(part)
        gates[part] = z[:, 2 * RET_DK + D_BR:]
        v_ref[rows, :] = z[:, 2 * RET_DK:2 * RET_DK + D_BR]
        cos2 = jnp.concatenate([cos_ref[rows, :]] * (RET_DK // LANES), axis=-1)
        sin2 = jnp.concatenate([sin_ref[rows, :]] * (RET_DK // LANES), axis=-1)

        def rope(t):
            partner = jnp.where(first2, pltpu.roll(t, RET_DK - ROPE_HALF, 1), pltpu.roll(t, ROPE_HALF, 1))
            return t * cos2 + partner * sin2

        yield
        q_ref[rows, :] = rope(z[:, 0:RET_DK])
        yield
        kr = rope(z[:, RET_DK:2 * RET_DK]) * (dk ** -0.5)
        k_ref[rows, :] = kr
        yield
        ke_ref[rows, :] = kr * jnp.concatenate([kw_ref[...]] * part_chunks, axis=0)
        yield

    def scores(part):
        atts[part], upds[part] = [], []
        for rows in chunks_of(part):
            atts[part].append(_mm_nt(q_ref[rows, :], _block_diag(k_ref[rows, :], 4, CHUNK, dk)) * dmat_ref[...])
            yield
            upds[part].append(_mm_tn(v_ref[rows, :], ke_ref[rows, :]))
            yield

    def states(part):
        st = st_ref[...]
        for rows, att, upd in zip(chunks_of(part), atts.pop(part), upds.pop(part)):
            y_ref[rows, :] = (_mm(att, _block_diag(v_ref[rows, :], 4, CHUNK, dv))
                              + _mm_nt(q_ref[rows, :], st) * qw_ref[...])
            st = st * cdec_ref[:, :RET_DK] + jnp.where(st_mask, upd, 0.0)
            yield
        st_ref[...] = st

    def finish(part):
        gate = gates.pop(part)
        for i, rows in enumerate(chunks_of(part)):
            o = y_ref[rows, :]
            mean = _seg_sum(o, dv) * (1.0 / dv)
            oc = o - mean
            var = _seg_sum(oc * oc, dv) * (1.0 / dv)
            o = oc * lax.rsqrt(var + NORM_EPS)
            o_ref[rows, :] = (o * gn_ref[...] * _silu(gate[i * CHUNK:(i + 1) * CHUNK])).astype(o_ref.dtype)
            yield

    yield from _pipeline([project, rotate, scores, states, finish], n_parts)


def _ret_tables():
    h = 4
    dk = RET_DK // h
    dv = D_BR // h
    log_g = jnp.log(1.0 - jnp.exp2(-5.0 - jnp.arange(h, dtype=F32)))
    idx = jnp.arange(CHUNK, dtype=F32)
    diff = idx[:, None] - idx[None, :]
    dmat = jnp.where(diff >= 0, jnp.exp(jnp.maximum(diff, 0.0)[None] * log_g[:, None, None]), 0.0)
    dmat = jnp.transpose(dmat, (1, 0, 2)).reshape(CHUNK, h * CHUNK)
    q_w = jnp.exp((idx + 1.0)[:, None] * log_g)
    k_w = jnp.exp((CHUNK - 1.0 - idx)[:, None] * log_g)
    cdec = jnp.exp(CHUNK * log_g)
    return dmat, jnp.repeat(q_w, dv, axis=1), jnp.repeat(k_w, dk, axis=1), _pad_row(jnp.repeat(cdec, dk))


def _prologue_kernel(x_ref, g_ref, pos_ref, freq_ref, un_ref, cos_ref, sin_ref):
    un_ref[...] = _rmsnorm(x_ref[...], g_ref[...]).astype(un_ref.dtype)
    ang = pos_ref[...].astype(F32) * freq_ref[...]
    cs = jnp.concatenate([jnp.cos(ang), jnp.sin(ang)], axis=0)
    er = _iota((2 * ROPE_HALF, 2 * LANES), 0)
    ec = _iota((2 * ROPE_HALF, 2 * LANES), 1)
    hit = (ec % ROPE_HALF) == (er % ROPE_HALF)
    sign = jnp.where((ec % (2 * ROPE_HALF)) < ROPE_HALF, -1.0, 1.0)
    spread = jnp.where(hit & (er < ROPE_HALF) & (ec < LANES), 1.0,
                       jnp.where(hit & (er >= ROPE_HALF) & (ec >= LANES), sign, 0.0)).astype(BF16)
    c_hi, c_mid, c_lo = _split3(cs)
    tab = _mm_tn(c_hi, spread) + (_mm_tn(c_mid, spread) + _mm_tn(c_lo, spread))
    cos_ref[...] = tab[:, :LANES]
    sin_ref[...] = tab[:, LANES:]


def _prologue(x, g, positions):
    bsz, s, _ = x.shape
    ts = _time_tile(s, PROLOGUE_TILE_SCALE)
    freq = ROPE_BASE ** (-jnp.arange(ROPE_HALF, dtype=F32) / ROPE_HALF)
    freq = jnp.broadcast_to(freq[:, None], (ROPE_HALF, ts))
    tile = lambda n: pl.BlockSpec((None, ts, n), lambda b, t: (b, t, 0))
    tab = jax.ShapeDtypeStruct((bsz, s, LANES), F32)
    return pl.pallas_call(
        _prologue_kernel,
        grid=(bsz, s // ts),
        in_specs=[tile(D_MODEL), _const_spec((1, D_MODEL)),
                  pl.BlockSpec((None, 1, ts), lambda b, t: (b, 0, t)), _const_spec((ROPE_HALF, ts))],
        out_specs=[tile(D_MODEL), tile(LANES), tile(LANES)],
        out_shape=[jax.ShapeDtypeStruct(x.shape, BF16), tab, tab],
        compiler_params=_compiler_params(2),
        name="norm_rope_prologue",
    )(x, g, positions[:, None, :], freq)


def _ret_kernel(*refs):
    _run(_ret_body(*refs))


def _ret_branch(un, cos, sin, w, gn_g):
    bsz, s, _ = un.shape
    ts = _time_tile(s, LIGHT_TILE_SCALE)
    dmat, q_w, k_w, cdec = _ret_tables()
    tab = pl.BlockSpec((None, ts, LANES), lambda b, t: (b, t, 0))
    vec = lambda n: _const_spec((1, n))
    return pl.pallas_call(
        _ret_kernel,
        grid=(bsz, s // ts),
        in_specs=[pl.BlockSpec((None, ts, D_MODEL), lambda b, t: (b, t, 0)),
                  tab, tab, _const_spec(w.shape), vec(D_BR),
                  _const_spec(dmat.shape), _const_spec(q_w.shape), _const_spec(k_w.shape), _const_spec(cdec.shape)],
        out_specs=pl.BlockSpec((None, ts, D_BR), lambda b, t: (b, t, 0)),
        out_shape=jax.ShapeDtypeStruct((bsz, s, D_BR), BF16),
        scratch_shapes=[pltpu.VMEM((D_BR, RET_DK), F32),
                        pltpu.VMEM((ts, RET_DK), F32), pltpu.VMEM((ts, RET_DK), F32),
                        pltpu.VMEM((ts, RET_DK), F32), pltpu.VMEM((ts, D_BR), F32),
                        pltpu.VMEM((ts, D_BR), F32)],
        compiler_params=_compiler_params(2),
        name="ret_branch",
    )(un, cos, sin, w, gn_g, dmat, q_w, k_w, cdec)


def _lru_merge_kernel(h_ref, un_ref, y0_ref, y1_ref, y2_ref, gx_ref, wm_ref, wb_ref, wo_ref,
                      wl_ref, cw_ref, cb_ref, wax_ref, bax_ref, lam_ref, o_ref, xn_ref, xbuf, hc_ref):
    ts = un_ref.shape[0]
    step = 2 * LANES

    @pl.when(pl.program_id(1) == 0)
    def _():
        xbuf[0:CARRY_ROWS, :] = jnp.zeros((CARRY_ROWS, D_BR), F32)
        hc_ref[...] = jnp.zeros(hc_ref.shape, F32)

    un = un_ref[...]
    terms = []
    out = {}

    def merge_matmuls():
        for n, y_ref in enumerate((y0_ref, y1_ref, y2_ref)):
            y = y_ref[...]
            cols = []
            for c0 in range(0, D_MODEL, step):
                gate = _sigmoid(_mm(un, wm_ref[:, n * D_MODEL + c0:n * D_MODEL + c0 + step]))
                yield
                cols.append(gate * _mm(y, wb_ref[n, :, c0:c0 + step]))
                yield
            terms.append(jnp.concatenate(cols, axis=-1))
        cols = []
        for c0 in range(0, D_MODEL, step):
            cols.append(_sigmoid(_mm(un, wm_ref[:, 3 * D_MODEL + c0:3 * D_MODEL + c0 + step])))
            yield
        out["gate3"] = jnp.concatenate(cols, axis=-1)

    def lru():
        z = _mm(un, wl_ref[...])
        yield
        gate = z[:, D_BR:]
        xbuf[CARRY_ROWS:CARRY_ROWS + ts, :] = z[:, :D_BR]
        xc = cb_ref[...] + jnp.zeros((ts, D_BR), F32)
        for j in range(LRU_CONV):
            off = CARRY_ROWS - (LRU_CONV - 1) + j
            xc = xc + xbuf[off:off + ts, :] * cw_ref[j:j + 1, :]
            yield
        xbuf[0:CARRY_ROWS, :] = xbuf[ts:ts + CARRY_ROWS, :]
        ri = _sigmoid(_mm(xc, wax_ref[...]) + bax_ref[...])
        yield
        log_a = -LRU_C * ri[:, :D_BR] * _softplus(-lam_ref[...])
        a = jnp.exp(log_a)
        yield
        u = jnp.sqrt(1.0 - jnp.exp(2.0 * log_a)) * (ri[:, D_BR:] * xc)
        yield
        sub = _iota((CARRY_ROWS, D_BR), 0)
        carry = hc_ref[0:1, :]
        blocks = []
        for r0 in range(0, ts, CARRY_ROWS):
            ab = a[r0:r0 + CARRY_ROWS]
            ub = u[r0:r0 + CARRY_ROWS]
            for d in (1, 2, 4):
                keep = sub >= d
                ub = jnp.where(keep, ab * pltpu.roll(ub, d, 0) + ub, ub)
                ab = jnp.where(keep, ab * pltpu.roll(ab, d, 0), ab)
            hb = ub + ab * carry
            carry = hb[CARRY_ROWS - 1:CARRY_ROWS, :]
            blocks.append(hb)
            if (r0 // CARRY_ROWS) % 4 == 3:
                yield
        hh = jnp.concatenate(blocks, axis=0)
        hc_ref[...] = jnp.broadcast_to(hh[ts - 1:ts, :], hc_ref.shape)
        out["y_lru"] = hh * _silu(gate)

    _run(_zip(merge_matmuls(), lru()))
    merged = terms[0] + terms[1] + terms[2] + out["gate3"] * _mm(out["y_lru"], wb_ref[3])
    res = h_ref[...] + _mm(merged, wo_ref[...])
    o_ref[...] = res
    xn_ref[...] = _rmsnorm(res, gx_ref[...]).astype(xn_ref.dtype)


def _lru_merge(h, un, ys, g_xa, w_merge, w_branch, w_out, w_lru, conv_w, conv_b, wax, bax, lam):
    bsz, s, _ = h.shape
    ts = _time_tile(s)
    vec = lambda n: _const_spec((1, n))
    tile = lambda n: pl.BlockSpec((None, ts, n), lambda b, t: (b, t, 0))
    return pl.pallas_call(
        _lru_merge_kernel,
        grid=(bsz, s // ts),
        in_specs=[tile(D_MODEL), tile(D_MODEL), tile(D_BR), tile(D_BR), tile(D_BR),
                  vec(D_MODEL), _const_spec(w_merge.shape), _const_spec(w_branch.shape), _const_spec(w_out.shape),
                  _const_spec(w_lru.shape), _const_spec(conv_w.shape), vec(D_BR), _const_spec(wax.shape),
                  vec(2 * D_BR), vec(D_BR)],
        out_specs=[tile(D_MODEL), tile(D_MODEL)],
        out_shape=[jax.ShapeDtypeStruct(h.shape, F32), jax.ShapeDtypeStruct(h.shape, BF16)],
        scratch_shapes=[pltpu.VMEM((ts + CARRY_ROWS, D_BR), F32), pltpu.VMEM((CARRY_ROWS, D_BR), F32)],
        compiler_params=_compiler_params(2),
        name="lru_merge_out",
    )(h, un, *ys, g_xa, w_merge, w_branch, w_out, w_lru, conv_w, conv_b, wax, bax, lam)


def _xattn_kernel(h_ref, xn_ref, mem_ref, gm_ref, wq_ref, wkv_ref, wo_ref, gn_ref, *rest, final_norm):
    if final_norm:
        o_ref, k_ref, v_ref = rest
    else:
        o_ref, un_ref, k_ref, v_ref = rest

    @pl.when(pl.program_id(1) == 0)
    def _():
        mn = _rmsnorm(mem_ref[...], gm_ref[...])
        kv = _mm(mn, wkv_ref[...])
        k_ref[...] = kv[:, :D_MODEL].astype(BF16)
        v_ref[...] = kv[:, D_MODEL:].astype(BF16)

    q = _mm(xn_ref[...], wq_ref[...])
    outs = []
    for hd in range(XA_HEADS):
        ls = slice(hd * XA_HD, (hd + 1) * XA_HD)
        sc = _mm_nt(q[:, ls], k_ref[:, ls]) * (XA_HD ** -0.5)
        sc = sc - jnp.max(sc, axis=-1, keepdims=True)
        e = jnp.exp(sc)
        pr = e / jnp.sum(e, axis=-1, keepdims=True)
        outs.append(_mm(pr, v_ref[:, ls]))
    out = h_ref[...] + _mm(jnp.concatenate(outs, axis=-1), wo_ref[...])
    normed = _rmsnorm(out, gn_ref[...])
    if final_norm:
        o_ref[...] = normed
    else:
        o_ref[...] = out
        un_ref[...] = normed.astype(un_ref.dtype)


def _xattn(h, xn, mem, g_mem, wq, wkv, wo, g_next, final_norm):
    bsz, s, _ = h.shape
    n_mem = mem.shape[1]
    ts = _time_tile(s, LIGHT_TILE_SCALE)
    vec = lambda n: _const_spec((1, n))
    tile = pl.BlockSpec((None, ts, D_MODEL), lambda b, t: (b, t, 0))
    out_f32 = jax.ShapeDtypeStruct(h.shape, F32)
    return pl.pallas_call(
        functools.partial(_xattn_kernel, final_norm=final_norm),
        grid=(bsz, s // ts),
        in_specs=[tile, tile, pl.BlockSpec((None, n_mem, D_MODEL), lambda b, t: (b, 0, 0)),
                  vec(D_MODEL), _const_spec(wq.shape), _const_spec(wkv.shape), _const_spec(wo.shape),
                  vec(D_MODEL)],
        out_specs=tile if final_norm else [tile, tile],
        out_shape=out_f32 if final_norm else [out_f32, jax.ShapeDtypeStruct(h.shape, BF16)],
        scratch_shapes=[pltpu.VMEM((n_mem, D_MODEL), BF16), pltpu.VMEM((n_mem, D_MODEL), BF16)],
        compiler_params=_compiler_params(2),
        name="xattn_final" if final_norm else "xattn",
    )(h, xn, mem, g_mem, wq, wkv, wo, g_next)


def _row(v):
    return v.reshape(1, -1).astype(F32)


def _pad_row(v):
    v = _row(v)
    return jnp.pad(v, ((0, 0), (0, -v.shape[1] % ROW_PAD_LANES)))


def _dense_block_diag(w):
    n, c, d = w.shape
    eye = jnp.eye(n, dtype=w.dtype)
    return (eye[:, None, :, None] * w[:, :, None, :]).reshape(n * c, n * d)


def kernel(x, mem, positions, mix_norm_g, w_in, rw_mu, rw_w0, rw_w2, rw_a0, rw_a2, rw_k_k, rw_k_a, rw_r_k, rw_ln_g, rw_ln_b, gla_f_up, gla_f_b, gla_norm_g, ret_gn_g, lru_conv_w, lru_conv_b, lru_wa, lru_ba, lru_wx, lru_bx, lru_lambda, w_branch, w_out, xa_norm_g, xa_mem_norm_g, xa_wq, xa_wkv, xa_wo, final_norm_g):
    depth = w_in.shape[0]
    n_rw = RW_SHIFT + D_BR
    n_gla = 2 * GLA_DK + 2 * D_BR + GLA_RANK
    n_ret = 2 * RET_DK + 2 * D_BR
    o_gla = n_rw
    o_ret = o_gla + n_gla
    o_lru = o_ret + n_ret
    o_mrg = o_lru + 2 * D_BR
    h = x
    un, cos, sin = _prologue(x, _row(mix_norm_g[0]), positions)
    for l in range(depth):
        wl = w_in[l]
        piece = lambda a, b: wl[:, a:b].astype(BF16)
        last = l == depth - 1

        z64 = jnp.zeros((RW_RANK, D_BR), F32)
        wlr = jnp.concatenate([jnp.concatenate([rw_w2[l], z64], axis=1),
                               jnp.concatenate([z64, rw_a2[l]], axis=1)], axis=0).astype(BF16)
        y_rw = _rwkv_branch(un, piece(0, n_rw), _pad_row(rw_mu[l]), wlr,
                            _row(jnp.concatenate([rw_w0[l], rw_a0[l]])), _row(rw_k_k[l]), _row(rw_k_a[l]),
                            _row(rw_r_k[l]), _row(rw_ln_g[l]), _row(rw_ln_b[l]))

        o_f = o_gla + 2 * GLA_DK + D_BR
        w_gla = jnp.concatenate([wl[:, o_gla:o_f], wl[:, o_f + GLA_RANK:o_ret], wl[:, o_f:o_f + GLA_RANK],
                                 jnp.zeros((D_MODEL, LANES - GLA_RANK), F32)], axis=1).astype(BF16)
        f_up = jnp.concatenate([gla_f_up[l], jnp.zeros((LANES - GLA_RANK, GLA_DK), F32)], axis=0).astype(BF16)
        y_gla = _gla_branch(un, w_gla, f_up, _pad_row(gla_f_b[l]), _row(gla_norm_g[l]))

        y_ret = _ret_branch(un, cos, sin, piece(o_ret, o_lru), _row(ret_gn_g[l]))

        wax = jnp.concatenate([_dense_block_diag(lru_wa[l]), _dense_block_diag(lru_wx[l])], axis=1).astype(BF16)
        h, xn = _lru_merge(h, un, (y_rw, y_gla, y_ret), _row(xa_norm_g[l]), piece(o_mrg, None),
                           w_branch[l].astype(BF16), w_out[l].astype(BF16), piece(o_lru, o_mrg),
                           lru_conv_w[l].astype(F32), _row(lru_conv_b[l]), wax,
                           _row(jnp.concatenate([lru_ba[l], lru_bx[l]])), _row(lru_lambda[l]))
        g_next = _row(final_norm_g if last else mix_norm_g[l + 1])
        res = _xattn(h, xn, mem, _row(xa_mem_norm_g[l]), xa_wq[l].astype(BF16), xa_wkv[l].astype(BF16),
                     xa_wo[l].astype(BF16), g_next, last)
        if last:
            return res
        h, un = res
```

```python
import functools

import jax
import jax.numpy as jnp
from jax import lax
from jax.experimental import pallas as pl
from jax.experimental.pallas import tpu as pltpu

F32 = jnp.float32
BF16 = jnp.bfloat16

D_MODEL = 1024
D_BR = 512
RW_HEADS = 8
RW_HD = D_BR // RW_HEADS
RW_RANK = 64
RW_LN_EPS = 64e-5
RW_SHIFT = 3 * D_BR + 2 * RW_RANK
GLA_DK = D_BR // 2
GLA_RANK = 16
GLA_TAU = 16.0
RET_DK = D_BR // 2
ROPE_BASE = 10000.0
ROPE_HALF = 32
LRU_CONV = 4
LRU_C = 8.0
XA_HEADS = 4
XA_HD = D_MODEL // XA_HEADS
NORM_EPS = 1e-6
CHUNK = 64
LANES = 128
CARRY_ROWS = 8
MAX_TIME_TILE = 512
PROLOGUE_TILE_SCALE = 4
LIGHT_TILE_SCALE = 2
VMEM_LIMIT_BYTES = 56 * 1024 * 1024
ROW_PAD_LANES = 512


def _mm(a, b):
    return jnp.dot(a.astype(BF16), b.astype(BF16), preferred_element_type=F32)


def _mm_nt(a, b):
    return lax.dot_general(a.astype(BF16), b.astype(BF16), (((1,), (1,)), ((), ())),
                           preferred_element_type=F32)


def _mm_tn(a, b):
    return lax.dot_general(a.astype(BF16), b.astype(BF16), (((0,), (0,)), ((), ())),
                           preferred_element_type=F32)


def _split2(x):
    hi = x.astype(BF16)
    lo = (x - hi.astype(F32)).astype(BF16)
    return hi, lo


def _split3(x):
    hi = x.astype(BF16)
    r = x - hi.astype(F32)
    mid = r.astype(BF16)
    lo = (r - mid.astype(F32)).astype(BF16)
    return hi, mid, lo


def _rmsnorm(x, g):
    ms = jnp.mean(x * x, axis=-1, keepdims=True)
    return x * lax.rsqrt(ms + NORM_EPS) * g


def _sigmoid(x):
    return 1.0 / (1.0 + jnp.exp(-x))


def _softplus(x):
    return jnp.maximum(x, 0.0) + jnp.log(1.0 + jnp.exp(-jnp.abs(x)))


def _silu(x):
    return x * _sigmoid(x)


def _iota(shape, dim):
    return lax.broadcasted_iota(jnp.int32, shape, dim)


def _seg_sum(x, width):
    pieces = []
    for j in range(x.shape[-1] // LANES):
        xs = x[:, j * LANES:(j + 1) * LANES]
        if width == LANES:
            s = jnp.sum(xs, axis=-1, keepdims=True)
            pieces.append(jnp.broadcast_to(s, xs.shape))
        else:
            low = _iota(xs.shape, 1) < width
            s_lo = jnp.sum(jnp.where(low, xs, 0.0), axis=-1, keepdims=True)
            s_hi = jnp.sum(jnp.where(low, 0.0, xs), axis=-1, keepdims=True)
            pieces.append(jnp.where(low, s_lo, s_hi))
    return jnp.concatenate(pieces, axis=-1)


def _block_diag(x, n, row_blk, col_blk):
    t = jnp.concatenate([x] * n, axis=0)
    keep = (_iota(t.shape, 0) // row_blk) == (_iota(t.shape, 1) // col_blk)
    return jnp.where(keep, t, jnp.zeros_like(t))


def _block_mask(shape, row_blk, col_blk):
    return (_iota(shape, 0) // row_blk) == (_iota(shape, 1) // col_blk)


def _chunk_cumsum(x):
    ts = x.shape[0]
    tri = jnp.where(_iota((CHUNK, CHUNK), 1) <= _iota((CHUNK, CHUNK), 0), 1.0, 0.0).astype(BF16)
    hi, lo = _split2(x)
    cums = []
    for r0 in range(0, ts, CHUNK):
        cums.append(_mm(tri, hi[r0:r0 + CHUNK]) + _mm(tri, lo[r0:r0 + CHUNK]))
    ends = jnp.concatenate([c[CHUNK - 1:CHUNK] for c in cums], axis=0)
    return cums, ends


def _const_spec(shape):
    nd = len(shape)
    return pl.BlockSpec(shape, lambda *_: (0,) * nd, pipeline_mode=pl.Buffered(1))


def _time_tile(s, scale=1):
    ts = min(MAX_TIME_TILE * scale, s)
    assert s % ts == 0 and ts % CHUNK == 0
    return ts


def _compiler_params(n_axes):
    return pltpu.CompilerParams(dimension_semantics=("arbitrary",) * n_axes,
                                vmem_limit_bytes=VMEM_LIMIT_BYTES)


def _zip(*gens):
    live = list(gens)
    while live:
        for gen in list(live):
            if next(gen, StopIteration) is StopIteration:
                live.remove(gen)
        yield


def _run(gen):
    for _ in gen:
        pass


def _pipeline(stage_fns, n_parts):
    n_stages = len(stage_fns)
    for step in range(n_stages + n_parts - 1):
        yield from _zip(*[stage_fns[step - part](part) for part in range(n_parts) if 0 <= step - part < n_stages])


def _rwkv_body(un_ref, w_ref, mu_ref, wlr_ref, w0a0_ref, kk_ref, ka_ref, rk_ref, lng_ref, lnb_ref,
                 o_ref, ht_ref, kt_ref, rt_ref, kd_ref, bd_ref, ke_ref, be_ref, v_ref, dec_ref,
                 wm_ref, u0_ref, akv_ref, ab_ref, y_ref, gate_ref, bonus_ref, zbuf):
    ts = un_ref.shape[0]
    hw = 4 * RW_HD
    n_parts = 2
    part_rows = ts // n_parts
    part_chunks = part_rows // CHUNK
    n_in = w_ref.shape[1]

    @pl.when(pl.program_id(1) == 0)
    def _():
        zbuf[0:CARRY_ROWS, :] = jnp.zeros((CARRY_ROWS, RW_SHIFT), F32)
        ht_ref[...] = jnp.zeros(ht_ref.shape, F32)

    col = _iota((CHUNK, hw), 1) % CHUNK
    row = _iota((CHUNK, hw), 0)
    strict = col < row
    incl = col <= row
    eye = jnp.where(col == row, 1.0, 0.0)
    blk = _block_mask((hw, hw), RW_HD, RW_HD)
    bdg = functools.partial(_block_diag, n=4, row_blk=CHUNK, col_blk=RW_HD)
    groups = [slice(hw * gi, hw * (gi + 1)) for gi in range(RW_HEADS // 4)]
    zs = {}

    def rows_of(part):
        return slice(part * part_rows, (part + 1) * part_rows)

    def chunks_of(part):
        return [(part * part_chunks + i, slice(part * part_rows + i * CHUNK, part * part_rows + (i + 1) * CHUNK))
                for i in range(part_chunks)]

    def project(part):
        un = un_ref[rows_of(part), :]
        cols = []
        for c0 in range(0, n_in, 2 * LANES):
            cols.append(_mm(un, w_ref[:, c0:min(c0 + 2 * LANES, n_in)]))
            yield
        zs[part] = jnp.concatenate(cols, axis=-1)

    def prepare(part):
        z = zs.pop(part)
        rows = rows_of(part)
        r0 = part * part_rows
        zsh = z[:, :RW_SHIFT]
        gate_ref[rows, :] = z[:, RW_SHIFT:]
        zbuf[CARRY_ROWS + r0:CARRY_ROWS + r0 + part_rows, :] = zsh
        prev = zbuf[CARRY_ROWS - 1 + r0:CARRY_ROWS - 1 + r0 + part_rows, :]
        if part == n_parts - 1:
            zbuf[0:CARRY_ROWS, :] = zbuf[ts:ts + CARRY_ROWS, :]
        sh = zsh + (prev - zsh) * mu_ref[:, :RW_SHIFT]
        yield
        r = sh[:, 0:D_BR]
        k = sh[:, D_BR:2 * D_BR]
        v = sh[:, 2 * D_BR:3 * D_BR]
        lr = sh[:, 3 * D_BR:]
        lr = jnp.where(_iota(lr.shape, 1) < RW_RANK, jnp.tanh(lr), lr)
        pre = _mm(lr, wlr_ref[...]) + w0a0_ref[...]
        yield
        log_w = -_softplus(-pre[:, :D_BR]) - 0.5
        lw = -jnp.exp(log_w)
        yield
        a = _sigmoid(pre[:, D_BR:])
        yield
        kk = k * kk_ref[...]
        kk = kk * lax.rsqrt(jnp.maximum(_seg_sum(kk * kk, RW_HD), 1e-24))
        yield
        k_mod = k * (1.0 + (a - 1.0) * ka_ref[...])
        b = kk * a
        bonus_ref[rows, :] = _seg_sum(r * k_mod * rk_ref[...], RW_HD) * v
        v_ref[rows, :] = v
        yield
        cums, ends = _chunk_cumsum(lw)
        yield
        c = jnp.concatenate(cums, axis=0)
        c_end = jnp.concatenate([jnp.broadcast_to(ends[i:i + 1], (CHUNK, D_BR)) for i in range(part_chunks)], axis=0)
        dec_ref[part * part_chunks:(part + 1) * part_chunks, :] = jnp.exp(ends)
        inv = jnp.exp(-c)
        kd_ref[rows, :] = k_mod * inv
        yield
        bd_ref[rows, :] = b * inv
        yield
        to_end = jnp.exp(c_end - c)
        ke_ref[rows, :] = k_mod * to_end
        yield
        be_ref[rows, :] = -(b * to_end)
        yield
        kt_ref[rows, :] = kk * jnp.exp(c - lw)
        yield
        rt_ref[rows, :] = r * jnp.exp(c)
        yield

    def solve(part):
        inst = [(rows, ls) for _, rows in chunks_of(part) for ls in groups]
        kts = [kt_ref[rows, ls] for rows, ls in inst]
        lhs = [jnp.concatenate([kt, rt_ref[rows, ls]], axis=0) for kt, (rows, ls) in zip(kts, inst)]
        sbs = [_mm_nt(x, bdg(bd_ref[rows, ls])) for x, (rows, ls) in zip(lhs, inst)]
        yield
        sks = [_mm_nt(x, bdg(kd_ref[rows, ls])) for x, (rows, ls) in zip(lhs, inst)]
        for sb, (rows, ls) in zip(sbs, inst):
            ab_ref[rows, ls] = jnp.where(incl, sb[CHUNK:], 0.0)
        yield
        lk_ak = [jnp.concatenate([jnp.where(strict, sk[:CHUNK], 0.0), jnp.where(incl, sk[CHUNK:], 0.0)], axis=0)
                 for sk in sks]
        ms = [jnp.where(strict, -sb[:CHUNK], 0.0) for sb in sbs]
        ps = [eye + m for m in ms]
        ms = [_mm(m, bdg(m)) for m in ms]
        yield
        for _ in range(4):
            sq = [_mm(jnp.concatenate([m, p], axis=0), bdg(m)) for m, p in zip(ms, ps)]
            ps = [p + x[CHUNK:] for p, x in zip(ps, sq)]
            ms = [x[:CHUNK] for x in sq]
            yield
        ps = [p + _mm(p, bdg(m)) for m, p in zip(ms, ps)]
        yield
        xs = [_mm(x, bdg(v_ref[rows, ls])) for x, (rows, ls) in zip(lk_ak, inst)]
        yield
        for p, kt, x, (rows, ls) in zip(ps, kts, xs, inst):
            akv_ref[rows, ls] = x[CHUNK:]
            wm_ref[rows, ls] = _mm(p, bdg(kt))
            u0_ref[rows, ls] = _mm(p, bdg(x[:CHUNK]))
        yield

    def scan(part):
        for ci, rows in chunks_of(part):
            hts = [ht_ref[gi] for gi in range(len(groups))]
            wrs = [_mm_nt(jnp.concatenate([wm_ref[rows, ls], rt_ref[rows, ls]], axis=0), ht)
                   for ht, ls in zip(hts, groups)]
            us = [wr[:CHUNK] + u0_ref[rows, ls] for wr, ls in zip(wrs, groups)]
            upds = [_mm_tn(jnp.concatenate([v_ref[rows, ls], u], axis=0),
                           jnp.concatenate([ke_ref[rows, ls], be_ref[rows, ls]], axis=0)) for u, ls in zip(us, groups)]
            for gi, (ht, upd, ls) in enumerate(zip(hts, upds, groups)):
                ht_ref[gi] = ht * dec_ref[ci:ci + 1, ls] + jnp.where(blk, upd, 0.0)
            for wr, u, ls in zip(wrs, us, groups):
                y_ref[rows, ls] = wr[CHUNK:] + (akv_ref[rows, ls] - _mm(ab_ref[rows, ls], bdg(u)))
            yield

    def finish(part):
        for _, rows in chunks_of(part):
            y = y_ref[rows, :]
            mean = _seg_sum(y, RW_HD) * (1.0 / RW_HD)
            yc = y - mean
            var = _seg_sum(yc * yc, RW_HD) * (1.0 / RW_HD)
            y = yc * lax.rsqrt(var + RW_LN_EPS) * lng_ref[...] + lnb_ref[...]
            o_ref[rows, :] = ((y + bonus_ref[rows, :]) * _silu(gate_ref[rows, :])).astype(o_ref.dtype)
            yield

    yield from _pipeline([project, prepare, solve, scan, finish], n_parts)


def _rwkv_kernel(*refs):
    _run(_rwkv_body(*refs))


def _rwkv_branch(un, w, mu, wlr, w0a0, k_k, k_a, r_k, ln_g, ln_b):
    bsz, s, _ = un.shape
    ts = _time_tile(s, LIGHT_TILE_SCALE)
    n_in = w.shape[1]
    vec = lambda n: _const_spec((1, n))
    big = pltpu.VMEM((ts, D_BR), F32)
    return pl.pallas_call(
        _rwkv_kernel,
        grid=(bsz, s // ts),
        in_specs=[pl.BlockSpec((None, ts, D_MODEL), lambda b, t: (b, t, 0)),
                  _const_spec((D_MODEL, n_in)), vec(mu.shape[1]),
                  _const_spec((2 * RW_RANK, 2 * D_BR)), vec(2 * D_BR),
                  vec(D_BR), vec(D_BR), vec(D_BR), vec(D_BR), vec(D_BR)],
        out_specs=pl.BlockSpec((None, ts, D_BR), lambda b, t: (b, t, 0)),
        out_shape=jax.ShapeDtypeStruct((bsz, s, D_BR), BF16),
        scratch_shapes=[pltpu.VMEM((RW_HEADS // 4, 4 * RW_HD, 4 * RW_HD), F32),
                        big, big, big, big, big, big, big, pltpu.VMEM((ts // CHUNK, D_BR), F32),
                        big, big, big, big, big, big, big,
                        pltpu.VMEM((ts + CARRY_ROWS, RW_SHIFT), F32)],
        compiler_params=_compiler_params(2),
        name="rwkv_branch",
    )(un, w, mu, wlr, w0a0, k_k, k_a, r_k, ln_g, ln_b)


def _gla_body(un_ref, w_ref, fup_ref, fb_ref, ng_ref, o_ref, st_ref, qd_ref, kd_ref, ke_ref, v_ref, y_ref, dec_ref):
    ts = un_ref.shape[0]
    dk = GLA_DK // 4
    dv = D_BR // 4
    n_parts = 2
    part_rows = ts // n_parts
    part_chunks = part_rows // CHUNK
    n_in = w_ref.shape[1]

    @pl.when(pl.program_id(1) == 0)
    def _():
        st_ref[...] = jnp.zeros(st_ref.shape, F32)

    causal = (_iota((CHUNK, 4 * CHUNK), 1) % CHUNK) <= _iota((CHUNK, 4 * CHUNK), 0)
    st_mask = _block_mask((D_BR, GLA_DK), dv, dk)
    zs, gates, atts, upds = {}, {}, {}, {}

    def rows_of(part):
        return slice(part * part_rows, (part + 1) * part_rows)

    def chunks_of(part):
        return [(part * part_chunks + i, slice(part * part_rows + i * CHUNK, part * part_rows + (i + 1) * CHUNK))
                for i in range(part_chunks)]

    def project(part):
        un = un_ref[rows_of(part), :]
        cols = []
        for c0 in range(0, n_in, 2 * LANES):
            cols.append(_mm(un, w_ref[:, c0:min(c0 + 2 * LANES, n_in)]))
            yield
        zs[part] = jnp.concatenate(cols, axis=-1)

    def decays(part):
        z = zs.pop(part)
        rows = rows_of(part)
        gates[part] = z[:, 2 * GLA_DK + D_BR:2 * GLA_DK + 2 * D_BR]
        f = _mm(z[:, 2 * GLA_DK + 2 * D_BR:], fup_ref[...]) + fb_ref[:, :GLA_DK]
        yield
        log_f = -_softplus(-f) * (1.0 / GLA_TAU)
        yield
        cums, ends = _chunk_cumsum(log_f)
        yield
        bcum = jnp.concatenate(cums, axis=0)
        b_end = jnp.concatenate([jnp.broadcast_to(ends[i:i + 1], (CHUNK, GLA_DK)) for i in range(part_chunks)], axis=0)
        qd_ref[rows, :] = z[:, 0:GLA_DK] * (dk ** -0.5) * jnp.exp(bcum)
        yield
        kd_ref[rows, :] = z[:, GLA_DK:2 * GLA_DK] * jnp.exp(-bcum)
        yield
        ke_ref[rows, :] = z[:, GLA_DK:2 * GLA_DK] * jnp.exp(b_end - bcum)
        yield
        v_ref[rows, :] = z[:, 2 * GLA_DK:2 * GLA_DK + D_BR]
        dec_ref[part * part_chunks:(part + 1) * part_chunks, :] = jnp.exp(ends)
        yield

    def scores(part):
        atts[part], upds[part] = [], []
        for _, rows in chunks_of(part):
            att = _mm_nt(qd_ref[rows, :], _block_diag(kd_ref[rows, :], 4, CHUNK, dk))
            atts[part].append(jnp.where(causal, att, 0.0))
            yield
            upds[part].append(_mm_tn(v_ref[rows, :], ke_ref[rows, :]))
            yield

    def states(part):
        st = st_ref[...]
        for (ci, rows), att, upd in zip(chunks_of(part), atts.pop(part), upds.pop(part)):
            y_ref[rows, :] = _mm(att, _block_diag(v_ref[rows, :], 4, CHUNK, dv)) + _mm_nt(qd_ref[rows, :], st)
            st = st * dec_ref[ci:ci + 1, :] + jnp.where(st_mask, upd, 0.0)
            yield
        st_ref[...] = st

    def finish(part):
        gate = gates.pop(part)
        for i, (_, rows) in enumerate(chunks_of(part)):
            o = y_ref[rows, :]
            o = o * lax.rsqrt(_seg_sum(o * o, dv) * (1.0 / dv) + NORM_EPS)
            o_ref[rows, :] = (o * ng_ref[...] * _silu(gate[i * CHUNK:(i + 1) * CHUNK])).astype(o_ref.dtype)
            yield

    yield from _pipeline([project, decays, scores, states, finish], n_parts)


def _gla_kernel(*refs):
    _run(_gla_body(*refs))


def _gla_branch(un, w, f_up, f_b, norm_g):
    bsz, s, _ = un.shape
    ts = _time_tile(s)
    vec = lambda n: _const_spec((1, n))
    return pl.pallas_call(
        _gla_kernel,
        grid=(bsz, s // ts),
        in_specs=[pl.BlockSpec((None, ts, D_MODEL), lambda b, t: (b, t, 0)),
                  _const_spec(w.shape), _const_spec(f_up.shape), vec(f_b.shape[1]), vec(D_BR)],
        out_specs=pl.BlockSpec((None, ts, D_BR), lambda b, t: (b, t, 0)),
        out_shape=jax.ShapeDtypeStruct((bsz, s, D_BR), BF16),
        scratch_shapes=[pltpu.VMEM((D_BR, GLA_DK), F32),
                        pltpu.VMEM((ts, GLA_DK), F32), pltpu.VMEM((ts, GLA_DK), F32),
                        pltpu.VMEM((ts, GLA_DK), F32), pltpu.VMEM((ts, D_BR), F32),
                        pltpu.VMEM((ts, D_BR), F32), pltpu.VMEM((ts // CHUNK, GLA_DK), F32)],
        compiler_params=_compiler_params(2),
        name="gla_branch",
    )(un, w, f_up, f_b, norm_g)


def _ret_body(un_ref, cos_ref, sin_ref, w_ref, gn_ref, dmat_ref, qw_ref, kw_ref, cdec_ref,
                o_ref, st_ref, q_ref, k_ref, ke_ref, v_ref, y_ref):
    ts = un_ref.shape[0]
    dk = RET_DK // 4
    dv = D_BR // 4
    n_parts = 2
    part_rows = ts // n_parts
    part_chunks = part_rows // CHUNK
    n_in = w_ref.shape[1]

    @pl.when(pl.program_id(1) == 0)
    def _():
        st_ref[...] = jnp.zeros(st_ref.shape, F32)

    first2 = (_iota((part_rows, RET_DK), 1) % (2 * ROPE_HALF)) < ROPE_HALF
    st_mask = _block_mask((D_BR, RET_DK), dv, dk)
    zs, gates, atts, upds = {}, {}, {}, {}

    def rows_of(part):
        return slice(part * part_rows, (part + 1) * part_rows)

    def chunks_of(part):
        return [slice(part * part_rows + i * CHUNK, part * part_rows + (i + 1) * CHUNK) for i in range(part_chunks)]

    def project(part):
        un = un_ref[rows_of(part), :]
        cols = []
        for c0 in range(0, n_in, 2 * LANES):
            cols.append(_mm(un, w_ref[:, c0:c0 + 2 * LANES]))
            yield
        zs[part] = jnp.concatenate(cols, axis=-1)

    def rotate(part):
        z = zs.pop(part)
        rows = rows_of(part)
        gates[part] = z[:, 2 * RET_DK + D_BR:]
        v_ref[rows, :] = z[:, 2 * RET_DK:2 * RET_DK + D_BR].astype(BF16)
        cos2 = jnp.concatenate([cos_ref[rows, :]] * (RET_DK // LANES), axis=-1)
        sin2 = jnp.concatenate([sin_ref[rows, :]] * (RET_DK // LANES), axis=-1)

        def rope(t):
            partner = jnp.where(first2, pltpu.roll(t, RET_DK - ROPE_HALF, 1), pltpu.roll(t, ROPE_HALF, 1))
            return t * cos2 + partner * sin2

        yield
        q_ref[rows, :] = rope(z[:, 0:RET_DK]).astype(BF16)
        yield
        kr = rope(z[:, RET_DK:2 * RET_DK]) * (dk ** -0.5)
        k_ref[rows, :] = kr.astype(BF16)
        yield
        ke_ref[rows, :] = (kr * jnp.concatenate([kw_ref[...]] * part_chunks, axis=0)).astype(BF16)
        yield

    def scores(part):
        atts[part], upds[part] = [], []
        for rows in chunks_of(part):
            atts[part].append(_mm_nt(q_ref[rows, :], _block_diag(k_ref[rows, :], 4, CHUNK, dk)) * dmat_ref[...])
            yield
            upds[part].append(_mm_tn(v_ref[rows, :], ke_ref[rows, :]))
            yield

    def states(part):
        st = st_ref[...]
        for rows, att, upd in zip(chunks_of(part), atts.pop(part), upds.pop(part)):
            y_ref[rows, :] = (_mm(att, _block_diag(v_ref[rows, :], 4, CHUNK, dv))
                              + _mm_nt(q_ref[rows, :], st) * qw_ref[...])
            st = st * cdec_ref[:, :RET_DK] + jnp.where(st_mask, upd, 0.0)
            yield
        st_ref[...] = st

    def finish(part):
        gate = gates.pop(part)
        for i, rows in enumerate(chunks_of(part)):
            o = y_ref[rows, :]
            mean = _seg_sum(o, dv) * (1.0 / dv)
            oc = o - mean
            var = _seg_sum(oc * oc, dv) * (1.0 / dv)
            o = oc * lax.rsqrt(var + NORM_EPS)
            o_ref[rows, :] = (o * gn_ref[...] * _silu(gate[i * CHUNK:(i + 1) * CHUNK])).astype(o_ref.dtype)
            yield

    yield from _pipeline([project, rotate, scores, states, finish], n_parts)


def _ret_tables():
    h = 4
    dk = RET_DK // h
    dv = D_BR // h
    log_g = jnp.log(1.0 - jnp.exp2(-5.0 - jnp.arange(h, dtype=F32)))
    idx = jnp.arange(CHUNK, dtype=F32)
    diff = idx[:, None] - idx[None, :]
    dmat = jnp.where(diff >= 0, jnp.exp(jnp.maximum(diff, 0.0)[None] * log_g[:, None, None]), 0.0)
    dmat = jnp.transpose(dmat, (1, 0, 2)).reshape(CHUNK, h * CHUNK)
    q_w = jnp.exp((idx + 1.0)[:, None] * log_g)
    k_w = jnp.exp((CHUNK - 1.0 - idx)[:, None] * log_g)
    cdec = jnp.exp(CHUNK * log_g)
    return dmat, jnp.repeat(q_w, dv, axis=1), jnp.repeat(k_w, dk, axis=1), _pad_row(jnp.repeat(cdec, dk))


def _prologue_kernel(x_ref, g_ref, pos_ref, freq_ref, un_ref, cos_ref, sin_ref):
    un_ref[...] = _rmsnorm(x_ref[...], g_ref[...]).astype(un_ref.dtype)
    ang = pos_ref[...].astype(F32) * freq_ref[...]
    cs = jnp.concatenate([jnp.cos(ang), jnp.sin(ang)], axis=0)
    er = _iota((2 * ROPE_HALF, 2 * LANES), 0)
    ec = _iota((2 * ROPE_HALF, 2 * LANES), 1)
    hit = (ec % ROPE_HALF) == (er % ROPE_HALF)
    sign = jnp.where((ec % (2 * ROPE_HALF)) < ROPE_HALF, -1.0, 1.0)
    spread = jnp.where(hit & (er < ROPE_HALF) & (ec < LANES), 1.0,
                       jnp.where(hit & (er >= ROPE_HALF) & (ec >= LANES), sign, 0.0)).astype(BF16)
    c_hi, c_mid, c_lo = _split3(cs)
    tab = _mm_tn(c_hi, spread) + (_mm_tn(c_mid, spread) + _mm_tn(c_lo, spread))
    cos_ref[...] = tab[:, :LANES]
    sin_ref[...] = tab[:, LANES:]


def _prologue(x, g, positions):
    bsz, s, _ = x.shape
    ts = _time_tile(s, PROLOGUE_TILE_SCALE)
    freq = ROPE_BASE ** (-jnp.arange(ROPE_HALF, dtype=F32) / ROPE_HALF)
    freq = jnp.broadcast_to(freq[:, None], (ROPE_HALF, ts))
    tile = lambda n: pl.BlockSpec((None, ts, n), lambda b, t: (b, t, 0))
    tab = jax.ShapeDtypeStruct((bsz, s, LANES), F32)
    return pl.pallas_call(
        _prologue_kernel,
        grid=(bsz, s // ts),
        in_specs=[tile(D_MODEL), _const_spec((1, D_MODEL)),
                  pl.BlockSpec((None, 1, ts), lambda b, t: (b, 0, t)), _const_spec((ROPE_HALF, ts))],
        out_specs=[tile(D_MODEL), tile(LANES), tile(LANES)],
        out_shape=[jax.ShapeDtypeStruct(x.shape, BF16), tab, tab],
        compiler_params=_compiler_params(2),
        name="norm_rope_prologue",
    )(x, g, positions[:, None, :], freq)


def _ret_kernel(*refs):
    _run(_ret_body(*refs))


def _ret_branch(un, cos, sin, w, gn_g):
    bsz, s, _ = un.shape
    ts = _time_tile(s, LIGHT_TILE_SCALE)
    dmat, q_w, k_w, cdec = _ret_tables()
    tab = pl.BlockSpec((None, ts, LANES), lambda b, t: (b, t, 0))
    vec = lambda n: _const_spec((1, n))
    return pl.pallas_call(
        _ret_kernel,
        grid=(bsz, s // ts),
        in_specs=[pl.BlockSpec((None, ts, D_MODEL), lambda b, t: (b, t, 0)),
                  tab, tab, _const_spec(w.shape), vec(D_BR),
                  _const_spec(dmat.shape), _const_spec(q_w.shape), _const_spec(k_w.shape), _const_spec(cdec.shape)],
        out_specs=pl.BlockSpec((None, ts, D_BR), lambda b, t: (b, t, 0)),
        out_shape=jax.ShapeDtypeStruct((bsz, s, D_BR), BF16),
        scratch_shapes=[pltpu.VMEM((D_BR, RET_DK), F32),
                        pltpu.VMEM((ts, RET_DK), BF16), pltpu.VMEM((ts, RET_DK), BF16),
                        pltpu.VMEM((ts, RET_DK), BF16), pltpu.VMEM((ts, D_BR), BF16),
                        pltpu.VMEM((ts, D_BR), F32)],
        compiler_params=_compiler_params(2),
        name="ret_branch",
    )(un, cos, sin, w, gn_g, dmat, q_w, k_w, cdec)


def _lru_merge_kernel(h_ref, un_ref, y0_ref, y1_ref, y2_ref, gx_ref, wm_ref, wb_ref, wo_ref,
                      wl_ref, cw_ref, cb_ref, wax_ref, bax_ref, lam_ref, o_ref, xn_ref, xbuf, hc_ref):
    ts = un_ref.shape[0]
    step = 2 * LANES

    @pl.when(pl.program_id(1) == 0)
    def _():
        xbuf[0:CARRY_ROWS, :] = jnp.zeros((CARRY_ROWS, D_BR), F32)
        hc_ref[...] = jnp.zeros(hc_ref.shape, F32)

    un = un_ref[...]
    terms = []
    out = {}

    def merge_matmuls():
        for n, y_ref in enumerate((y0_ref, y1_ref, y2_ref)):
            y = y_ref[...]
            cols = []
            for c0 in range(0, D_MODEL, step):
                gate = _sigmoid(_mm(un, wm_ref[:, n * D_MODEL + c0:n * D_MODEL + c0 + step]))
                yield
                cols.append(gate * _mm(y, wb_ref[n, :, c0:c0 + step]))
                yield
            terms.append(jnp.concatenate(cols, axis=-1))
        cols = []
        for c0 in range(0, D_MODEL, step):
            cols.append(_sigmoid(_mm(un, wm_ref[:, 3 * D_MODEL + c0:3 * D_MODEL + c0 + step])))
            yield
        out["gate3"] = jnp.concatenate(cols, axis=-1)

    def lru():
        z = _mm(un, wl_ref[...])
        yield
        gate = z[:, D_BR:]
        xbuf[CARRY_ROWS:CARRY_ROWS + ts, :] = z[:, :D_BR]
        xc = cb_ref[...] + jnp.zeros((ts, D_BR), F32)
        for j in range(LRU_CONV):
            off = CARRY_ROWS - (LRU_CONV - 1) + j
            xc = xc + xbuf[off:off + ts, :] * cw_ref[j:j + 1, :]
            yield
        xbuf[0:CARRY_ROWS, :] = xbuf[ts:ts + CARRY_ROWS, :]
        ri = _sigmoid(_mm(xc, wax_ref[...]) + bax_ref[...])
        yield
        log_a = -LRU_C * ri[:, :D_BR] * _softplus(-lam_ref[...])
        a = jnp.exp(log_a)
        yield
        u = jnp.sqrt(1.0 - jnp.exp(2.0 * log_a)) * (ri[:, D_BR:] * xc)
        yield
        sub = _iota((CARRY_ROWS, D_BR), 0)
        carry = hc_ref[0:1, :]
        blocks = []
        for r0 in range(0, ts, CARRY_ROWS):
            ab = a[r0:r0 + CARRY_ROWS]
            ub = u[r0:r0 + CARRY_ROWS]
            for d in (1, 2, 4):
                keep = sub >= d
                ub = jnp.where(keep, ab * pltpu.roll(ub, d, 0) + ub, ub)
                ab = jnp.where(keep, ab * pltpu.roll(ab, d, 0), ab)
            hb = ub + ab * carry
            carry = hb[CARRY_ROWS - 1:CARRY_ROWS, :]
            blocks.append(hb)
            if (r0 // CARRY_ROWS) % 4 == 3:
                yield
        hh = jnp.concatenate(blocks, axis=0)
        hc_ref[...] = jnp.broadcast_to(hh[ts - 1:ts, :], hc_ref.shape)
        out["y_lru"] = hh * _silu(gate)

    _run(_zip(merge_matmuls(), lru()))
    merged = terms[0] + terms[1] + terms[2] + out["gate3"] * _mm(out["y_lru"], wb_ref[3])
    res = h_ref[...] + _mm(merged, wo_ref[...])
    o_ref[...] = res
    xn_ref[...] = _rmsnorm(res, gx_ref[...]).astype(xn_ref.dtype)


def _lru_merge(h, un, ys, g_xa, w_merge, w_branch, w_out, w_lru, conv_w, conv_b, wax, bax, lam):
    bsz, s, _ = h.shape
    ts = _time_tile(s)
    vec = lambda n: _const_spec((1, n))
    tile = lambda n: pl.BlockSpec((None, ts, n), lambda b, t: (b, t, 0))
    return pl.pallas_call(
        _lru_merge_kernel,
        grid=(bsz, s // ts),
        in_specs=[tile(D_MODEL), tile(D_MODEL), tile(D_BR), tile(D_BR), tile(D_BR),
                  vec(D_MODEL), _const_spec(w_merge.shape), _const_spec(w_branch.shape), _const_spec(w_out.shape),
                  _const_spec(w_lru.shape), _const_spec(conv_w.shape), vec(D_BR), _const_spec(wax.shape),
                  vec(2 * D_BR), vec(D_BR)],
        out_specs=[tile(D_MODEL), tile(D_MODEL)],
        out_shape=[jax.ShapeDtypeStruct(h.shape, F32), jax.ShapeDtypeStruct(h.shape, BF16)],
        scratch_shapes=[pltpu.VMEM((ts + CARRY_ROWS, D_BR), F32), pltpu.VMEM((CARRY_ROWS, D_BR), F32)],
        compiler_params=_compiler_params(2),
        name="lru_merge_out",
    )(h, un, *ys, g_xa, w_merge, w_branch, w_out, w_lru, conv_w, conv_b, wax, bax, lam)


def _xattn_kernel(h_ref, xn_ref, mem_ref, gm_ref, wq_ref, wkv_ref, wo_ref, gn_ref, *rest, final_norm):
    if final_norm:
        o_ref, k_ref, v_ref = rest
    else:
        o_ref, un_ref, k_ref, v_ref = rest

    @pl.when(pl.program_id(1) == 0)
    def _():
        mn = _rmsnorm(mem_ref[...], gm_ref[...])
        kv = _mm(mn, wkv_ref[...])
        k_ref[...] = kv[:, :D_MODEL].astype(BF16)
        v_ref[...] = kv[:, D_MODEL:].astype(BF16)

    q = _mm(xn_ref[...], wq_ref[...])
    outs = []
    for hd in range(XA_HEADS):
        ls = slice(hd * XA_HD, (hd + 1) * XA_HD)
        sc = _mm_nt(q[:, ls], k_ref[:, ls]) * (XA_HD ** -0.5)
        sc = sc - jnp.max(sc, axis=-1, keepdims=True)
        e = jnp.exp(sc)
        pr = e / jnp.sum(e, axis=-1, keepdims=True)
        outs.append(_mm(pr, v_ref[:, ls]))
    out = h_ref[...] + _mm(jnp.concatenate(outs, axis=-1), wo_ref[...])
    normed = _rmsnorm(out, gn_ref[...])
    if final_norm:
        o_ref[...] = normed
    else:
        o_ref[...] = out
        un_ref[...] = normed.astype(un_ref.dtype)


def _xattn(h, xn, mem, g_mem, wq, wkv, wo, g_next, final_norm):
    bsz, s, _ = h.shape
    n_mem = mem.shape[1]
    ts = _time_tile(s, LIGHT_TILE_SCALE)
    vec = lambda n: _const_spec((1, n))
    tile = pl.BlockSpec((None, ts, D_MODEL), lambda b, t: (b, t, 0))
    out_f32 = jax.ShapeDtypeStruct(h.shape, F32)
    return pl.pallas_call(
        functools.partial(_xattn_kernel, final_norm=final_norm),
        grid=(bsz, s // ts),
        in_specs=[tile, tile, pl.BlockSpec((None, n_mem, D_MODEL), lambda b, t: (b, 0, 0)),
                  vec(D_MODEL), _const_spec(wq.shape), _const_spec(wkv.shape), _const_spec(wo.shape),
                  vec(D_MODEL)],
        out_specs=tile if final_norm else [tile, tile],
        out_shape=out_f32 if final_norm else [out_f32, jax.ShapeDtypeStruct(h.shape, BF16)],
        scratch_shapes=[pltpu.VMEM((n_mem, D_MODEL), BF16), pltpu.VMEM((n_mem, D_MODEL), BF16)],
        compiler_params=_compiler_params(2),
        name="xattn_final" if final_norm else "xattn",
    )(h, xn, mem, g_mem, wq, wkv, wo, g_next)


def _row(v):
    return v.reshape(1, -1).astype(F32)


def _pad_row(v):
    v = _row(v)
    return jnp.pad(v, ((0, 0), (0, -v.shape[1] % ROW_PAD_LANES)))


def _dense_block_diag(w):
    n, c, d = w.shape
    eye = jnp.eye(n, dtype=w.dtype)
    return (eye[:, None, :, None] * w[:, :, None, :]).reshape(n * c, n * d)


def kernel(x, mem, positions, mix_norm_g, w_in, rw_mu, rw_w0, rw_w2, rw_a0, rw_a2, rw_k_k, rw_k_a, rw_r_k, rw_ln_g, rw_ln_b, gla_f_up, gla_f_b, gla_norm_g, ret_gn_g, lru_conv_w, lru_conv_b, lru_wa, lru_ba, lru_wx, lru_bx, lru_lambda, w_branch, w_out, xa_norm_g, xa_mem_norm_g, xa_wq, xa_wkv, xa_wo, final_norm_g):
    depth = w_in.shape[0]
    n_rw = RW_SHIFT + D_BR
    n_gla = 2 * GLA_DK + 2 * D_BR + GLA_RANK
    n_ret = 2 * RET_DK + 2 * D_BR
    o_gla = n_rw
    o_ret = o_gla + n_gla
    o_lru = o_ret + n_ret
    o_mrg = o_lru + 2 * D_BR
    h = x
    un, cos, sin = _prologue(x, _row(mix_norm_g[0]), positions)
    for l in range(depth):
        wl = w_in[l]
        piece = lambda a, b: wl[:, a:b].astype(BF16)
        last = l == depth - 1

        z64 = jnp.zeros((RW_RANK, D_BR), F32)
        wlr = jnp.concatenate([jnp.concatenate([rw_w2[l], z64], axis=1),
                               jnp.concatenate([z64, rw_a2[l]], axis=1)], axis=0).astype(BF16)
        y_rw = _rwkv_branch(un, piece(0, n_rw), _pad_row(rw_mu[l]), wlr,
                            _row(jnp.concatenate([rw_w0[l], rw_a0[l]])), _row(rw_k_k[l]), _row(rw_k_a[l]),
                            _row(rw_r_k[l]), _row(rw_ln_g[l]), _row(rw_ln_b[l]))

        o_f = o_gla + 2 * GLA_DK + D_BR
        w_gla = jnp.concatenate([wl[:, o_gla:o_f], wl[:, o_f + GLA_RANK:o_ret], wl[:, o_f:o_f + GLA_RANK],
                                 jnp.zeros((D_MODEL, LANES - GLA_RANK), F32)], axis=1).astype(BF16)
        f_up = jnp.concatenate([gla_f_up[l], jnp.zeros((LANES - GLA_RANK, GLA_DK), F32)], axis=0).astype(BF16)
        y_gla = _gla_branch(un, w_gla, f_up, _pad_row(gla_f_b[l]), _row(gla_norm_g[l]))

        y_ret = _ret_branch(un, cos, sin, piece(o_ret, o_lru), _row(ret_gn_g[l]))

        wax = jnp.concatenate([_dense_block_diag(lru_wa[l]), _dense_block_diag(lru_wx[l])], axis=1).astype(BF16)
        h, xn = _lru_merge(h, un, (y_rw, y_gla, y_ret), _row(xa_norm_g[l]), piece(o_mrg, None),
                           w_branch[l].astype(BF16), w_out[l].astype(BF16), piece(o_lru, o_mrg),
                           lru_conv_w[l].astype(F32), _row(lru_conv_b[l]), wax,
                           _row(jnp.concatenate([lru_ba[l], lru_bx[l]])), _row(lru_lambda[l]))
        g_next = _row(final_norm_g if last else mix_norm_g[l + 1])
        res = _xattn(h, xn, mem, _row(xa_mem_norm_g[l]), xa_wq[l].astype(BF16), xa_wkv[l].astype(BF16),
                     xa_wo[l].astype(BF16), g_next, last)
        if last:
            return res
        h, un = res
```
